```python
import math
import jax, jax.numpy as jnp
from jax import lax
import numpy as np

D_MODEL = 1024
BATCH = 8
SEQ = 4096
DEPTH = 1

HEAD_DIM = 64
N_ATTN_HEADS = D_MODEL // (2 * HEAD_DIM)
ATTN_WIDTH = N_ATTN_HEADS * 2 * HEAD_DIM
CONV_WIDTH = D_MODEL
CONV_K = 3
D_FF = 2816
ROPE_THETA = 10000.0
Q_BLOCK = 128
EPS = 1e-6
SUBLN_EPS = 1e-5
IN_COLS = 3 * ATTN_WIDTH + 3 * CONV_WIDTH

kernel_name = "hybrid_diffattn_shortconv_convffn_adaln"


def rms_norm(x, g, eps=EPS):
    x32 = x.astype(jnp.float32)
    y = x32 * lax.rsqrt(jnp.mean(x32 * x32, axis=-1, keepdims=True) + eps)
    return (y * g.astype(jnp.float32)).astype(x.dtype)


def modulate(n, shift, scale):
    return n * (1.0 + scale) + shift


def rope_tables(positions):
    inv_freq = ROPE_THETA ** (-jnp.arange(0, HEAD_DIM, 2, dtype=jnp.float32) / HEAD_DIM)
    ang = positions.astype(jnp.float32)[..., None] * inv_freq
    return jnp.cos(ang), jnp.sin(ang)


def apply_rope(t, cos, sin):
    t32 = t.astype(jnp.float32)
    c = cos[:, :, None, None, :]
    s = sin[:, :, None, None, :]
    t1, t2 = jnp.split(t32, 2, axis=-1)
    return jnp.concatenate([t1 * c - t2 * s, t2 * c + t1 * s], axis=-1).astype(t.dtype)


def depthwise_conv3(u, w):
    up = jnp.pad(u, ((0, 0), (1, 1), (0, 0)))
    return up[:, :-2] * w[0] + up[:, 1:-1] * w[1] + up[:, 2:] * w[2]


def diff_attention(q, k, v, lam):
    b_, s_, h_, _, dh = q.shape
    nblk = s_ // Q_BLOCK
    qb = q.reshape(b_, nblk, Q_BLOCK, h_, 2, dh).swapaxes(0, 1)
    scale = dh ** -0.5

    def block(q_blk):
        s = jnp.einsum('bqhcd,bkhcd->bhcqk', q_blk, k).astype(jnp.float32) * scale
        p = jax.nn.softmax(s, axis=-1)
        p_diff = p[:, :, 0] - lam * p[:, :, 1]
        return jnp.einsum('bhqk,bkhe->bqhe', p_diff.astype(v.dtype), v)

    out = lax.map(block, qb)
    return out.swapaxes(0, 1).reshape(b_, s_, h_, 2 * dh)


def hybrid_mixer(h, cos, sin, w_in, conv_w, lq1, lk1, lq2, lk2, subln_g,
                 w_attn_o, w_conv_o, w_gate, b_gate, w_out, lambda_init):
    b_, s_, _ = h.shape
    proj = h @ w_in
    a0 = ATTN_WIDTH
    q, k, v, gb, gc, u = jnp.split(
        proj, [a0, 2 * a0, 3 * a0, 3 * a0 + CONV_WIDTH, 3 * a0 + 2 * CONV_WIDTH], axis=-1)
    q = apply_rope(q.reshape(b_, s_, N_ATTN_HEADS, 2, HEAD_DIM), cos, sin)
    k = apply_rope(k.reshape(b_, s_, N_ATTN_HEADS, 2, HEAD_DIM), cos, sin)
    v = v.reshape(b_, s_, N_ATTN_HEADS, 2 * HEAD_DIM)
    f32 = jnp.float32
    lam = (jnp.exp(jnp.sum(lq1.astype(f32) * lk1.astype(f32)))
           - jnp.exp(jnp.sum(lq2.astype(f32) * lk2.astype(f32))) + lambda_init)
    attn = diff_attention(q, k, v, lam)
    attn = rms_norm(attn, subln_g, SUBLN_EPS) * (1.0 - lambda_init)
    y_a = attn.reshape(b_, s_, ATTN_WIDTH) @ w_attn_o
    y_b = (gb * depthwise_conv3(gc * u, conv_w)) @ w_conv_o
    gate_a, gate_b = jnp.split(jax.nn.sigmoid(h @ w_gate + b_gate), 2, axis=-1)
    return (gate_a * y_a + gate_b * y_b) @ w_out


def conv_ffn(h, w_up, conv_w, conv_b, w_down):
    a, b = jnp.split(h @ w_up, 2, axis=-1)
    a = depthwise_conv3(a, conv_w) + conv_b
    return (jax.nn.silu(a) * b) @ w_down


def setup_inputs(seed: int = 0) -> dict:
    key = jax.random.key(seed)
    ks = jax.random.split(key, 26)
    L, D = DEPTH, D_MODEL
    f32 = jnp.float32

    def nrm(k, shape, fan_in, mult=1.0):
        return jax.random.normal(k, shape, f32) * (mult * fan_in ** -0.5)

    def gain(k, shape):
        return 1.0 + 0.05 * jax.random.normal(k, shape, f32)

    def small(k, shape, s=0.02):
        return s * jax.random.normal(k, shape, f32)

    return {
        "x": jax.random.normal(ks[0], (BATCH, SEQ, D), f32),
        "c": jax.random.normal(ks[1], (BATCH, D), f32),
        "positions": jnp.broadcast_to(jnp.arange(SEQ, dtype=jnp.int32), (BATCH, SEQ)),
        "w_ada": nrm(ks[2], (L, D, 6 * D), D, 0.5),
        "b_ada": small(ks[3], (L, 6 * D)),
        "norm1_g": gain(ks[4], (L, D)),
        "w_in": nrm(ks[5], (L, D, IN_COLS), D),
        "conv_w": nrm(ks[6], (L, CONV_K, CONV_WIDTH), CONV_K),
        "lambda_q1": small(ks[7], (L, HEAD_DIM), 0.1),
        "lambda_k1": small(ks[8], (L, HEAD_DIM), 0.1),
        "lambda_q2": small(ks[9], (L, HEAD_DIM), 0.1),
        "lambda_k2": small(ks[10], (L, HEAD_DIM), 0.1),
        "subln_g": gain(ks[11], (L, 2 * HEAD_DIM)),
        "w_attn_o": nrm(ks[12], (L, ATTN_WIDTH, D), ATTN_WIDTH),
        "w_conv_o": nrm(ks[13], (L, CONV_WIDTH, D), CONV_WIDTH),
        "w_gate": nrm(ks[14], (L, D, 2 * D), D),
        "b_gate": small(ks[15], (L, 2 * D)),
        "w_out": nrm(ks[16], (L, D, D), D),
        "norm2_g": gain(ks[17], (L, D)),
        "w_up": nrm(ks[18], (L, D, 2 * D_FF), D),
        "ffn_conv_w": nrm(ks[19], (L, CONV_K, D_FF), CONV_K),
        "ffn_conv_b": small(ks[20], (L, D_FF)),
        "w_down": nrm(ks[21], (L, D_FF, D), D_FF),
        "final_g": gain(ks[22], (D,)),
    }


def reference(x, c, positions, w_ada, b_ada, norm1_g, w_in, conv_w, lambda_q1, lambda_k1,
              lambda_q2, lambda_k2, subln_g, w_attn_o, w_conv_o, w_gate, b_gate, w_out,
              norm2_g, w_up, ffn_conv_w, ffn_conv_b, w_down, final_g):
    cos, sin = rope_tables(positions)
    c_act = jax.nn.silu(c)
    for layer in range(DEPTH):
        lambda_init = 0.8 - 0.6 * math.exp(-0.3 * layer)
        mod = (c_act @ w_ada[layer] + b_ada[layer])[:, None, :]
        sh1, sc1, g1, sh2, sc2, g2 = jnp.split(mod, 6, axis=-1)
        h = modulate(rms_norm(x, norm1_g[layer]), sh1, sc1)
        mix = hybrid_mixer(h, cos, sin, w_in[layer], conv_w[layer],
                           lambda_q1[layer], lambda_k1[layer], lambda_q2[layer], lambda_k2[layer],
                           subln_g[layer], w_attn_o[layer], w_conv_o[layer],
                           w_gate[layer], b_gate[layer], w_out[layer], lambda_init)
        x = x + g1 * mix
        h = modulate(rms_norm(x, norm2_g[layer]), sh2, sc2)
        x = x + g2 * conv_ffn(h, w_up[layer], ffn_conv_w[layer], ffn_conv_b[layer], w_down[layer])
    return rms_norm(x, final_g)
```

```python
import functools
import math

import jax
import jax.numpy as jnp
from jax import lax
from jax.experimental import pallas as pl
from jax.experimental.pallas import tpu as pltpu

F32 = jnp.float32
BF16 = jnp.bfloat16

HEAD_DIM = 64
HEAD_WIDTH = 2 * HEAD_DIM
ROPE_THETA = 10000.0
EPS = 1e-6
SUBLN_EPS = 1e-5
LANES = 128
BF16_SUBLANES = 16
VMEM_LIMIT = 56 * 1024 * 1024

ROW_TILE = 512
Q_TILE = 256
FF_CHUNKS = 2


def _params(*sem):
    return pltpu.CompilerParams(dimension_semantics=sem, vmem_limit_bytes=VMEM_LIMIT)


def _rms(x, g, eps):
    return x * lax.rsqrt(jnp.mean(x * x, axis=-1, keepdims=True) + eps) * g


def _resident(shape):
    return pl.BlockSpec(shape, lambda *_: (0,) * len(shape), pipeline_mode=pl.Buffered(1))


def _rope_table_kernel(pos_ref, inv_ref, cos_ref, sin_ref):
    ang = pos_ref[...].astype(F32) * inv_ref[...]
    cos_ref[...] = jnp.cos(ang)
    sin_ref[...] = jnp.sin(ang)


def _rope_tables(positions):
    rows = positions.size
    half = HEAD_DIM // 2
    per_row = LANES // half
    inv_freq = ROPE_THETA ** (-jnp.arange(0, HEAD_DIM, 2, dtype=F32) / HEAD_DIM)
    pos = jnp.repeat(positions.reshape(rows // per_row, per_row), half, axis=1)
    inv = jnp.tile(inv_freq, per_row).reshape(1, LANES)
    n = rows // per_row
    blk = min(n, 1024)
    cos, sin = pl.pallas_call(
        _rope_table_kernel,
        grid=(n // blk,),
        in_specs=[pl.BlockSpec((blk, LANES), lambda i: (i, 0)),
                  pl.BlockSpec((1, LANES), lambda i: (0, 0))],
        out_specs=[pl.BlockSpec((blk, LANES), lambda i: (i, 0))] * 2,
        out_shape=[jax.ShapeDtypeStruct((n, LANES), F32)] * 2,
        compiler_params=_params("parallel"),
        name="rope_table",
    )(pos, inv)
    cos = cos.reshape(rows, half)
    sin = sin.reshape(rows, half)
    cos128 = jnp.tile(cos, (1, per_row))
    sin128 = jnp.concatenate([-sin, sin] * (per_row // 2), axis=1)
    return cos128, sin128


def _ada_kernel(c_ref, w_ref, b_ref, o_ref):
    c = c_ref[...]
    c_act = (c * jax.nn.sigmoid(c)).astype(BF16)
    o_ref[...] = jnp.dot(c_act, w_ref[...].astype(BF16), preferred_element_type=F32) + b_ref[...]


def _ada_mod(c, w_ada, b_ada):
    bsz, d = c.shape
    n = w_ada.shape[1]
    return pl.pallas_call(
        _ada_kernel,
        grid=(n // d,),
        in_specs=[pl.BlockSpec((bsz, d), lambda j: (0, 0)),
                  pl.BlockSpec((d, d), lambda j: (0, j)),
                  pl.BlockSpec((1, d), lambda j: (0, j))],
        out_specs=pl.BlockSpec((bsz, d), lambda j: (0, j)),
        out_shape=jax.ShapeDtypeStruct((bsz, n), F32),
        compiler_params=_params("parallel"),
        name="ada_mod",
    )(c, w_ada, b_ada.reshape(1, n))


def _rope(t, cos, sin_signed):
    lane = lax.broadcasted_iota(jnp.int32, (1, LANES), 1)
    first_half = (lane % HEAD_DIM) < (HEAD_DIM // 2)
    out = []
    for cb in range(t.shape[1] // LANES):
        blk = t[:, cb * LANES:(cb + 1) * LANES]
        upper = pltpu.roll(blk, LANES - HEAD_DIM // 2, 1)
        lower = pltpu.roll(blk, HEAD_DIM // 2, 1)
        partner = jnp.where(first_half, upper, lower)
        out.append(blk * cos + partner * sin_signed)
    return jnp.concatenate(out, axis=1)


def _in_proj_kernel(x_ref, mod_ref, g_ref, w_ref, bg_ref, cos_ref, sin_ref,
                    q_ref, k_ref, v_ref, gb_ref, gcu_ref, gate_ref,
                    h_ref, acc_ref, gc_ref, *, d):
    j = pl.program_id(1)

    @pl.when(j == 0)
    def _():
        n = _rms(x_ref[...], g_ref[...], EPS)
        h = n * (1.0 + mod_ref[0, 1:2, :]) + mod_ref[0, 0:1, :]
        h_ref[...] = h.astype(BF16)

    acc_ref[...] = jnp.dot(h_ref[...], w_ref[...], preferred_element_type=F32)

    @pl.when(j == 0)
    def _():
        q = _rope(acc_ref[...], cos_ref[...], sin_ref[...]) * (HEAD_DIM ** -0.5)
        q_ref[...] = q.astype(BF16)

    @pl.when(j == 1)
    def _():
        k_ref[...] = _rope(acc_ref[...], cos_ref[...], sin_ref[...]).astype(BF16)

    @pl.when(j == 2)
    def _():
        v_ref[...] = acc_ref[...].astype(BF16)

    @pl.when(j == 3)
    def _():
        gb_ref[...] = acc_ref[...].astype(BF16)

    @pl.when(j == 4)
    def _():
        gc_ref[...] = acc_ref[...]

    @pl.when(j == 5)
    def _():
        gcu_ref[...] = (gc_ref[...] * acc_ref[...]).astype(BF16)

    @pl.when(j == 6)
    def _():
        gate_ref[:, :d] = jax.nn.sigmoid(acc_ref[...] + bg_ref[:, :d]).astype(BF16)

    @pl.when(j == 7)
    def _():
        gate_ref[:, d:] = jax.nn.sigmoid(acc_ref[...] + bg_ref[:, d:]).astype(BF16)


def _in_proj(x2, mod3, norm_g, w_cat, b_gate, cos128, sin128, seq):
    rows, d = x2.shape
    ts = ROW_TILE
    nb = seq // ts
    ncol = w_cat.shape[1] // d
    assert ncol == 8 and seq % ts == 0
    row_blk = lambda w: pl.BlockSpec((ts, w), lambda i, j: (i, 0))
    return pl.pallas_call(
        functools.partial(_in_proj_kernel, d=d),
        grid=(rows // ts, ncol),
        in_specs=[row_blk(d),
                  pl.BlockSpec((1, 6, d), lambda i, j: (i // nb, 0, 0)),
                  pl.BlockSpec((1, d), lambda i, j: (0, 0)),
                  pl.BlockSpec((d, d), lambda i, j: (0, j)),
                  pl.BlockSpec((1, 2 * d), lambda i, j: (0, 0)),
                  row_blk(LANES), row_blk(LANES)],
        out_specs=[row_blk(d)] * 5 + [row_blk(2 * d)],
        out_shape=[jax.ShapeDtypeStruct((rows, d), BF16)] * 5
                  + [jax.ShapeDtypeStruct((rows, 2 * d), BF16)],
        scratch_shapes=[pltpu.VMEM((ts, d), BF16),
                        pltpu.VMEM((ts, d), F32),
                        pltpu.VMEM((ts, d), F32)],
        compiler_params=_params("parallel", "arbitrary"),
        name="in_proj",
    )(x2, mod3, norm_g, w_cat, b_gate, cos128, sin128)


def _attn_kernel(q_ref, k_ref, v_ref, lq1_ref, lk1_ref, lq2_ref, lk2_ref, g_ref, o_ref,
                 *, lambda_init):
    bq = q_ref.shape[0]
    lam = (jnp.exp(jnp.sum(lq1_ref[...] * lk1_ref[...], axis=-1, keepdims=True))
           - jnp.exp(jnp.sum(lq2_ref[...] * lk2_ref[...], axis=-1, keepdims=True))
           + lambda_init)
    q = q_ref[...]
    lane = lax.broadcasted_iota(jnp.int32, (1, HEAD_WIDTH), 1)
    zero = jnp.zeros_like(q)
    qq = jnp.concatenate([jnp.where(lane < HEAD_DIM, q, zero),
                          jnp.where(lane >= HEAD_DIM, q, zero)], axis=0)
    s = lax.dot_general(qq, k_ref[...], (((1,), (1,)), ((), ())), preferred_element_type=F32)
    e = jnp.exp(s - jnp.max(s, axis=-1, keepdims=True))
    r = 1.0 / jnp.sum(e, axis=-1, keepdims=True)
    p = e[:bq] * r[:bq] - e[bq:] * (r[bq:] * lam)
    o = jnp.dot(p.astype(BF16), v_ref[...], preferred_element_type=F32)
    o_ref[...] = (_rms(o, g_ref[...], SUBLN_EPS) * (1.0 - lambda_init)).astype(BF16)


def _diff_attn(q, k, v, lq1, lk1, lq2, lk2, subln_g, batch, seq, lambda_init):
    rows, width = q.shape
    heads = width // HEAD_WIDTH
    bq = Q_TILE
    nq = seq // bq
    lam_spec = pl.BlockSpec((1, HEAD_DIM), lambda b, h, i: (0, 0))
    kv_spec = pl.BlockSpec((seq, HEAD_WIDTH), lambda b, h, i: (b, h))
    q_spec = pl.BlockSpec((bq, HEAD_WIDTH), lambda b, h, i: (b * nq + i, h))
    return pl.pallas_call(
        functools.partial(_attn_kernel, lambda_init=lambda_init),
        grid=(batch, heads, nq),
        in_specs=[q_spec, kv_spec, kv_spec, lam_spec, lam_spec, lam_spec, lam_spec,
                  pl.BlockSpec((1, HEAD_WIDTH), lambda b, h, i: (0, 0))],
        out_specs=q_spec,
        out_shape=jax.ShapeDtypeStruct((rows, width), BF16),
        compiler_params=_params("parallel", "parallel", "arbitrary"),
        name="diff_attn",
    )(q, k, v, lq1, lk1, lq2, lk2, subln_g)


def _conv3(cur, prev_row, next_row, w_ref):
    n = cur.shape[0]
    row = lax.broadcasted_iota(jnp.int32, (n, 1), 0)
    before = jnp.where(row == 0, prev_row, pltpu.roll(cur, 1, 0))
    after = jnp.where(row == n - 1, next_row, pltpu.roll(cur, n - 1, 0))
    return before * w_ref[0:1, :] + cur * w_ref[1:2, :] + after * w_ref[2:3, :]


def _mix_kernel(attn_ref, gb_ref, gcu_ref, gcu_prev_ref, gcu_next_ref, gate_ref, x_ref, mod_ref,
                cw_ref, wa_ref, wc_ref, wo_ref, g2_ref, x1_ref, h2_ref, *, d, tiles_per_seq):
    t = pl.program_id(0) % tiles_per_seq
    halo = gcu_prev_ref.shape[0]
    prev_row = gcu_prev_ref[...].astype(F32)[halo - 1:halo, :]
    next_row = gcu_next_ref[...].astype(F32)[0:1, :]
    prev_row = jnp.where(t == 0, 0.0, prev_row)
    next_row = jnp.where(t == tiles_per_seq - 1, 0.0, next_row)
    conv = _conv3(gcu_ref[...].astype(F32), prev_row, next_row, cw_ref)
    y_b_in = (gb_ref[...].astype(F32) * conv).astype(BF16)
    y_a = jnp.dot(attn_ref[...], wa_ref[...], preferred_element_type=F32)
    y_b = jnp.dot(y_b_in, wc_ref[...], preferred_element_type=F32)
    merged = gate_ref[:, :d].astype(F32) * y_a + gate_ref[:, d:].astype(F32) * y_b
    mix = jnp.dot(merged.astype(BF16), wo_ref[...], preferred_element_type=F32)
    x1 = x_ref[...] + mod_ref[0, 2:3, :] * mix
    x1_ref[...] = x1
    n = _rms(x1, g2_ref[...], EPS)
    h2_ref[...] = (n * (1.0 + mod_ref[0, 4:5, :]) + mod_ref[0, 3:4, :]).astype(BF16)


def _halo_specs(ts, width, rows, halo):
    per_tile = ts // halo
    last = rows // halo - 1
    prev = pl.BlockSpec((halo, width), lambda i: (jnp.maximum(i * per_tile - 1, 0), 0))
    nxt = pl.BlockSpec((halo, width), lambda i: (jnp.minimum((i + 1) * per_tile, last), 0))
    return prev, nxt


def _mix_out(attn, gb, gcu, gates, x2, mod3, conv_w, w_attn_o, w_conv_o, w_out, norm_g, seq):
    rows, d = x2.shape
    ts = ROW_TILE
    nb = seq // ts
    row_blk = lambda w: pl.BlockSpec((ts, w), lambda i: (i, 0))
    prev, nxt = _halo_specs(ts, d, rows, BF16_SUBLANES)
    return pl.pallas_call(
        functools.partial(_mix_kernel, d=d, tiles_per_seq=nb),
        grid=(rows // ts,),
        in_specs=[row_blk(d), row_blk(d), row_blk(d), prev, nxt, row_blk(2 * d), row_blk(d),
                  pl.BlockSpec((1, 6, d), lambda i: (i // nb, 0, 0)),
                  _resident((3, d)), _resident((d, d)), _resident((d, d)), _resident((d, d)),
                  _resident((1, d))],
        out_specs=[row_blk(d), row_blk(d)],
        out_shape=[jax.ShapeDtypeStruct((rows, d), F32), jax.ShapeDtypeStruct((rows, d), BF16)],
        compiler_params=_params("parallel"),
        name="mix_out",
    )(attn, gb, gcu, gcu, gcu, gates, x2, mod3, conv_w, w_attn_o, w_conv_o, w_out, norm_g)


def _ffn_kernel(h_ref, h_prev_ref, h_next_ref, x1_ref, mod_ref, wup_ref, cw_ref, cb_ref, wdn_ref,
                gf_ref, o_ref, *, d_ff, tiles_per_seq):
    t = pl.program_id(0) % tiles_per_seq
    ts = h_ref.shape[0]
    halo = h_prev_ref.shape[0]
    h = h_ref[...]
    h_prev = jnp.where(t == 0, jnp.zeros_like(h_prev_ref[...]), h_prev_ref[...])
    h_next = jnp.where(t == tiles_per_seq - 1, jnp.zeros_like(h_next_ref[...]), h_next_ref[...])
    h_ext = jnp.concatenate([h_prev, h, h_next], axis=0)
    n_ext = ts + 2 * halo
    chunk = d_ff // FF_CHUNKS
    ffn = jnp.zeros((ts, h.shape[1]), F32)
    for c in range(FF_CHUNKS):
        lo = c * chunk
        a = jnp.dot(h_ext, wup_ref[:, lo:lo + chunk], preferred_element_type=F32)
        b = jnp.dot(h, wup_ref[:, d_ff + lo:d_ff + lo + chunk], preferred_element_type=F32)
        before = pltpu.roll(a, 1, 0)[halo:halo + ts]
        after = pltpu.roll(a, n_ext - 1, 0)[halo:halo + ts]
        a = (before * cw_ref[0:1, lo:lo + chunk] + a[halo:halo + ts] * cw_ref[1:2, lo:lo + chunk]
             + after * cw_ref[2:3, lo:lo + chunk] + cb_ref[:, lo:lo + chunk])
        g = (a * jax.nn.sigmoid(a) * b).astype(BF16)
        ffn = ffn + jnp.dot(g, wdn_ref[lo:lo + chunk, :], preferred_element_type=F32)
    x2 = x1_ref[...] + mod_ref[0, 5:6, :] * ffn
    o_ref[...] = _rms(x2, gf_ref[...], EPS)


def _conv_ffn(h2, x1, mod3, w_up, conv_w, conv_b, w_down, final_g, seq):
    rows, d = x1.shape
    d_ff = w_down.shape[0]
    ts = ROW_TILE
    nb = seq // ts
    assert d_ff % (FF_CHUNKS * LANES) == 0
    row_blk = pl.BlockSpec((ts, d), lambda i: (i, 0))
    prev, nxt = _halo_specs(ts, d, rows, BF16_SUBLANES)
    return pl.pallas_call(
        functools.partial(_ffn_kernel, d_ff=d_ff, tiles_per_seq=nb),
        grid=(rows // ts,),
        in_specs=[row_blk, prev, nxt, row_blk,
                  pl.BlockSpec((1, 6, d), lambda i: (i // nb, 0, 0)),
                  _resident((d, 2 * d_ff)), _resident((3, d_ff)), _resident((1, d_ff)),
                  _resident((d_ff, d)), _resident((1, d))],
        out_specs=row_blk,
        out_shape=jax.ShapeDtypeStruct((rows, d), F32),
        compiler_params=_params("parallel"),
        name="conv_ffn",
    )(h2, h2, h2, x1, mod3, w_up, conv_w, conv_b, w_down, final_g)


def kernel(x, c, positions, w_ada, b_ada, norm1_g, w_in, conv_w, lambda_q1, lambda_k1, lambda_q2,
           lambda_k2, subln_g, w_attn_o, w_conv_o, w_gate, b_gate, w_out, norm2_g, w_up,
           ffn_conv_w, ffn_conv_b, w_down, final_g):
    batch, seq, d = x.shape
    depth = w_in.shape[0]
    rows = batch * seq
    cos128, sin128 = _rope_tables(positions)
    xf = x.reshape(rows, d)
    for layer in range(depth):
        lambda_init = 0.8 - 0.6 * math.exp(-0.3 * layer)
        mod3 = _ada_mod(c, w_ada[layer], b_ada[layer]).reshape(batch, 6, d)
        w_cat = jnp.concatenate([w_in[layer], w_gate[layer]], axis=1).astype(BF16)
        q, k, v, gb, gcu, gates = _in_proj(
            xf, mod3, norm1_g[layer].reshape(1, d), w_cat, b_gate[layer].reshape(1, 2 * d),
            cos128, sin128, seq)
        attn = _diff_attn(q, k, v,
                          lambda_q1[layer].reshape(1, HEAD_DIM), lambda_k1[layer].reshape(1, HEAD_DIM),
                          lambda_q2[layer].reshape(1, HEAD_DIM), lambda_k2[layer].reshape(1, HEAD_DIM),
                          subln_g[layer].reshape(1, HEAD_WIDTH), batch, seq, lambda_init)
        x1, h2 = _mix_out(attn, gb, gcu, gates, xf, mod3, conv_w[layer],
                          w_attn_o[layer].astype(BF16), w_conv_o[layer].astype(BF16),
                          w_out[layer].astype(BF16), norm2_g[layer].reshape(1, d), seq)
        last = layer == depth - 1
        assert last, "conv_ffn fuses the final RMSNorm: DEPTH > 1 needs an un-normalised variant"
        xf = _conv_ffn(h2, x1, mod3, w_up[layer].astype(BF16), ffn_conv_w[layer],
                       ffn_conv_b[layer].reshape(1, -1), w_down[layer].astype(BF16),
                       final_g.reshape(1, d), seq)
    return xf.reshape(batch, seq, d)
```

```python
import functools
import math

import jax
import jax.numpy as jnp
from jax import lax
from jax.experimental import pallas as pl
from jax.experimental.pallas import tpu as pltpu

F32 = jnp.float32
BF16 = jnp.bfloat16

HEAD_DIM = 64
HEAD_WIDTH = 2 * HEAD_DIM
ROPE_THETA = 10000.0
EPS = 1e-6
SUBLN_EPS = 1e-5
LANES = 128
BF16_SUBLANES = 16
VMEM_LIMIT = 56 * 1024 * 1024

ROW_TILE = 512
Q_TILE = 256
K_CHUNK = 512
FF_CHUNKS = 2


def _params(*sem):
    return pltpu.CompilerParams(dimension_semantics=sem, vmem_limit_bytes=VMEM_LIMIT)


def _rms(x, g, eps):
    return x * lax.rsqrt(jnp.mean(x * x, axis=-1, keepdims=True) + eps) * g


def _resident(shape):
    return pl.BlockSpec(shape, lambda *_: (0,) * len(shape), pipeline_mode=pl.Buffered(1))


def _rope_table_kernel(pos_ref, inv_ref, cos_ref, sin_ref):
    ang = pos_ref[...].astype(F32) * inv_ref[...]
    cos_ref[...] = jnp.cos(ang)
    sin_ref[...] = jnp.sin(ang)


def _rope_tables(positions):
    rows = positions.size
    half = HEAD_DIM // 2
    per_row = LANES // half
    inv_freq = ROPE_THETA ** (-jnp.arange(0, HEAD_DIM, 2, dtype=F32) / HEAD_DIM)
    pos = jnp.repeat(positions.reshape(rows // per_row, per_row), half, axis=1)
    inv = jnp.tile(inv_freq, per_row).reshape(1, LANES)
    n = rows // per_row
    blk = min(n, 1024)
    cos, sin = pl.pallas_call(
        _rope_table_kernel,
        grid=(n // blk,),
        in_specs=[pl.BlockSpec((blk, LANES), lambda i: (i, 0)),
                  pl.BlockSpec((1, LANES), lambda i: (0, 0))],
        out_specs=[pl.BlockSpec((blk, LANES), lambda i: (i, 0))] * 2,
        out_shape=[jax.ShapeDtypeStruct((n, LANES), F32)] * 2,
        compiler_params=_params("parallel"),
        name="rope_table",
    )(pos, inv)
    cos = cos.reshape(rows, half)
    sin = sin.reshape(rows, half)
    cos128 = jnp.tile(cos, (1, per_row))
    sin128 = jnp.concatenate([-sin, sin] * (per_row // 2), axis=1)
    return cos128, sin128


def _ada_kernel(c_ref, w_ref, b_ref, o_ref):
    c = c_ref[...]
    c_act = (c * jax.nn.sigmoid(c)).astype(BF16)
    o_ref[...] = jnp.dot(c_act, w_ref[...].astype(BF16), preferred_element_type=F32) + b_ref[...]


def _ada_mod(c, w_ada, b_ada):
    bsz, d = c.shape
    n = w_ada.shape[1]
    return pl.pallas_call(
        _ada_kernel,
        grid=(n // d,),
        in_specs=[pl.BlockSpec((bsz, d), lambda j: (0, 0)),
                  pl.BlockSpec((d, d), lambda j: (0, j)),
                  pl.BlockSpec((1, d), lambda j: (0, j))],
        out_specs=pl.BlockSpec((bsz, d), lambda j: (0, j)),
        out_shape=jax.ShapeDtypeStruct((bsz, n), F32),
        compiler_params=_params("parallel"),
        name="ada_mod",
    )(c, w_ada, b_ada.reshape(1, n))


def _rope(t, cos, sin_signed):
    lane = lax.broadcasted_iota(jnp.int32, (1, LANES), 1)
    first_half = (lane % HEAD_DIM) < (HEAD_DIM // 2)
    out = []
    for cb in range(t.shape[1] // LANES):
        blk = t[:, cb * LANES:(cb + 1) * LANES]
        upper = pltpu.roll(blk, LANES - HEAD_DIM // 2, 1)
        lower = pltpu.roll(blk, HEAD_DIM // 2, 1)
        partner = jnp.where(first_half, upper, lower)
        out.append(blk * cos + partner * sin_signed)
    return jnp.concatenate(out, axis=1)


def _in_proj_kernel(x_ref, mod_ref, g_ref, win_ref, wg_ref, bg_ref, cos_ref, sin_ref,
                    q_ref, k_ref, v_ref, gb_ref, gcu_ref, gate_ref, *, d):
    n = _rms(x_ref[...], g_ref[...], EPS)
    h = (n * (1.0 + mod_ref[0, 1:2, :]) + mod_ref[0, 0:1, :]).astype(BF16)

    def proj(j):
        return jnp.dot(h, win_ref[:, j * d:(j + 1) * d], preferred_element_type=F32)

    cos, sin = cos_ref[...], sin_ref[...]
    q_ref[...] = (_rope(proj(0), cos, sin) * (HEAD_DIM ** -0.5 * math.log2(math.e))).astype(BF16)
    k_ref[...] = _rope(proj(1), cos, sin).astype(BF16)
    v_ref[...] = proj(2).astype(BF16)
    gb_ref[...] = proj(3).astype(BF16)
    gcu_ref[...] = (proj(4) * proj(5)).astype(BF16)
    for j in range(2):
        z = jnp.dot(h, wg_ref[:, j * d:(j + 1) * d], preferred_element_type=F32)
        gate_ref[:, j * d:(j + 1) * d] = jax.nn.sigmoid(z + bg_ref[:, j * d:(j + 1) * d]).astype(BF16)


def _in_proj(x2, mod3, norm_g, w_in, w_gate, b_gate, cos128, sin128, seq):
    rows, d = x2.shape
    ts = ROW_TILE
    nb = seq // ts
    assert w_in.shape[1] == 6 * d and w_gate.shape[1] == 2 * d and seq % ts == 0
    row_blk = lambda w: pl.BlockSpec((ts, w), lambda i: (i, 0))
    return pl.pallas_call(
        functools.partial(_in_proj_kernel, d=d),
        grid=(rows // ts,),
        in_specs=[row_blk(d),
                  pl.BlockSpec((1, 6, d), lambda i: (i // nb, 0, 0)),
                  _resident((1, d)), _resident((d, 6 * d)), _resident((d, 2 * d)),
                  _resident((1, 2 * d)),
                  row_blk(LANES), row_blk(LANES)],
        out_specs=[row_blk(d)] * 5 + [row_blk(2 * d)],
        out_shape=[jax.ShapeDtypeStruct((rows, d), BF16)] * 5
                  + [jax.ShapeDtypeStruct((rows, 2 * d), BF16)],
        compiler_params=_params("parallel"),
        name="in_proj",
    )(x2, mod3, norm_g, w_in, w_gate, b_gate, cos128, sin128)


def _attn_kernel(q_ref, k_ref, v_ref, lq1_ref, lk1_ref, lq2_ref, lk2_ref, g_ref, o_ref,
                 vt_ref, s0_ref, s1_ref, *, lambda_init, bq, kc):
    seq = q_ref.shape[0]
    nq, nk = seq // bq, seq // kc
    nt_dims = (((1,), (1,)), ((), ()))
    s_refs = (s0_ref, s1_ref)
    lam =(jnp.exp(jnp.sum(lq1_ref[...] * lk1_ref[...], axis=-1, keepdims=True))
           - jnp.exp(jnp.sum(lq2_ref[...] * lk2_ref[...], axis=-1, keepdims=True))
           + lambda_init)
    for c in range(nk):
        vt_ref[c] = v_ref[c * kc:(c + 1) * kc, :].T
    lane = lax.broadcasted_iota(jnp.int32, (1, HEAD_WIDTH), 1)
    m_init = jnp.full((1, 2 * bq), -jnp.inf, F32)
    l_init = jnp.zeros((1, 2 * bq), F32)

    def masked_queries(t):
        q = q_ref[pl.ds(pl.multiple_of(t * bq, bq), bq), :]
        zero = jnp.zeros_like(q)
        return jnp.concatenate([jnp.where(lane < HEAD_DIM, q, zero),
                                jnp.where(lane >= HEAD_DIM, q, zero)], axis=0)

    def finish(t, l, acc):
        r = 1.0 / l
        o_t = acc[:, :bq] * r[:, :bq] - acc[:, bq:] * (r[:, bq:] * lam)
        o = _rms(o_t.T, g_ref[...], SUBLN_EPS) * (1.0 - lambda_init)
        o_ref[pl.ds(pl.multiple_of(t * bq, bq), bq), :] = o.astype(BF16)

    def tile_step(score_tile, score_slot, value_tile, m_prev):
        qq = None if score_tile is None else masked_queries(score_tile)
        m, l, acc = m_init, l_init, None
        for c in range(nk):
            if score_tile is not None:
                s = lax.dot_general(k_ref[c * kc:(c + 1) * kc, :], qq, nt_dims,
                                    preferred_element_type=F32)
                s_refs[score_slot][c] = s
                m = jnp.maximum(m, jnp.max(s, axis=0, keepdims=True))
            if value_tile is not None:
                e = jnp.exp2(s_refs[1 - score_slot][c] - m_prev)
                pv = jnp.dot(vt_ref[c], e.astype(BF16), preferred_element_type=F32)
                acc = pv if acc is None else acc + pv
                l = l + jnp.sum(e, axis=0, keepdims=True)
        if value_tile is not None:
            finish(value_tile, l, acc)
        return m

    def tile_pair(u, m_prev):
        m_odd = tile_step(2 * u + 1, 1, 2 * u, m_prev)
        return tile_step(2 * u + 2, 0, 2 * u + 1, m_odd)

    assert nq % 2 == 0
    m_prev = tile_step(0, 0, None, None)
    m_prev = lax.fori_loop(0, nq // 2 - 1, tile_pair, m_prev)
    m_prev = tile_step(nq - 1, 1, nq - 2, m_prev)
    tile_step(None, 0, nq - 1, m_prev)


def _diff_attn(q, k, v, lq1, lk1, lq2, lk2, subln_g, batch, seq, lambda_init):
    rows, width = q.shape
    heads = width // HEAD_WIDTH
    bq, kc = Q_TILE, K_CHUNK
    assert seq % bq == 0 and seq % kc == 0
    lam_spec = pl.BlockSpec((1, HEAD_DIM), lambda b, h: (0, 0))
    head_spec = pl.BlockSpec((seq, HEAD_WIDTH), lambda b, h: (b, h))
    return pl.pallas_call(
        functools.partial(_attn_kernel, lambda_init=lambda_init, bq=bq, kc=kc),
        grid=(batch, heads),
        in_specs=[head_spec, head_spec, head_spec, lam_spec, lam_spec, lam_spec, lam_spec,
                  pl.BlockSpec((1, HEAD_WIDTH), lambda b, h: (0, 0))],
        out_specs=head_spec,
        out_shape=jax.ShapeDtypeStruct((rows, width), BF16),
        scratch_shapes=[pltpu.VMEM((seq // kc, HEAD_WIDTH, kc), BF16),
                        pltpu.VMEM((seq // kc, kc, 2 * bq), F32),
                        pltpu.VMEM((seq // kc, kc, 2 * bq), F32)],
        compiler_params=_params("parallel", "parallel"),
        name="diff_attn",
    )(q, k, v, lq1, lk1, lq2, lk2, subln_g)


def _conv3(cur, prev_row, next_row, w_ref):
    n = cur.shape[0]
    row = lax.broadcasted_iota(jnp.int32, (n, 1), 0)
    before = jnp.where(row == 0, prev_row, pltpu.roll(cur, 1, 0))
    after = jnp.where(row == n - 1, next_row, pltpu.roll(cur, n - 1, 0))
    return before * w_ref[0:1, :] + cur * w_ref[1:2, :] + after * w_ref[2:3, :]


def _mix_kernel(attn_ref, gb_ref, gcu_ref, gcu_prev_ref, gcu_next_ref, gate_ref, x_ref, mod_ref,
                cw_ref, wa_ref, wc_ref, wo_ref, g2_ref, x1_ref, h2_ref, *, d, tiles_per_seq):
    t = pl.program_id(0) % tiles_per_seq
    halo = gcu_prev_ref.shape[0]
    prev_row = gcu_prev_ref[...].astype(F32)[halo - 1:halo, :]
    next_row = gcu_next_ref[...].astype(F32)[0:1, :]
    prev_row = jnp.where(t == 0, 0.0, prev_row)
    next_row = jnp.where(t == tiles_per_seq - 1, 0.0, next_row)
    conv = _conv3(gcu_ref[...].astype(F32), prev_row, next_row, cw_ref)
    y_b_in = (gb_ref[...].astype(F32) * conv).astype(BF16)
    y_a = jnp.dot(attn_ref[...], wa_ref[...], preferred_element_type=F32)
    y_b = jnp.dot(y_b_in, wc_ref[...], preferred_element_type=F32)
    merged = gate_ref[:, :d].astype(F32) * y_a + gate_ref[:, d:].astype(F32) * y_b
    mix = jnp.dot(merged.astype(BF16), wo_ref[...], preferred_element_type=F32)
    x1 = x_ref[...] + mod_ref[0, 2:3, :] * mix
    x1_ref[...] = x1
    n = _rms(x1, g2_ref[...], EPS)
    h2_ref[...] = (n * (1.0 + mod_ref[0, 4:5, :]) + mod_ref[0, 3:4, :]).astype(BF16)


def _halo_specs(ts, width, rows, halo):
    per_tile = ts // halo
    last = rows // halo - 1
    prev = pl.BlockSpec((halo, width), lambda i: (jnp.maximum(i * per_tile - 1, 0), 0))
    nxt = pl.BlockSpec((halo, width), lambda i: (jnp.minimum((i + 1) * per_tile, last), 0))
    return prev, nxt


def _mix_out(attn, gb, gcu, gates, x2, mod3, conv_w, w_attn_o, w_conv_o, w_out, norm_g, seq):
    rows, d = x2.shape
    ts = ROW_TILE
    nb = seq // ts
    row_blk = lambda w: pl.BlockSpec((ts, w), lambda i: (i, 0))
    prev, nxt = _halo_specs(ts, d, rows, BF16_SUBLANES)
    return pl.pallas_call(
        functools.partial(_mix_kernel, d=d, tiles_per_seq=nb),
        grid=(rows // ts,),
        in_specs=[row_blk(d), row_blk(d), row_blk(d), prev, nxt, row_blk(2 * d), row_blk(d),
                  pl.BlockSpec((1, 6, d), lambda i: (i // nb, 0, 0)),
                  _resident((3, d)), _resident((d, d)), _resident((d, d)), _resident((d, d)),
                  _resident((1, d))],
        out_specs=[row_blk(d), row_blk(d)],
        out_shape=[jax.ShapeDtypeStruct((rows, d), F32), jax.ShapeDtypeStruct((rows, d), BF16)],
        compiler_params=_params("parallel"),
        name="mix_out",
    )(attn, gb, gcu, gcu, gcu, gates, x2, mod3, conv_w, w_attn_o, w_conv_o, w_out, norm_g)


def _ffn_kernel(h_ref, h_prev_ref, h_next_ref, x1_ref, mod_ref, wup_ref, cw_ref, cb_ref, wdn_ref,
                gf_ref, o_ref, *, d_ff, tiles_per_seq):
    t = pl.program_id(0) % tiles_per_seq
    ts = h_ref.shape[0]
    halo = h_prev_ref.shape[0]
    h = h_ref[...]
    h_prev = jnp.where(t == 0, jnp.zeros_like(h_prev_ref[...]), h_prev_ref[...])
    h_next = jnp.where(t == tiles_per_seq - 1, jnp.zeros_like(h_next_ref[...]), h_next_ref[...])
    h_ext = jnp.concatenate([h_prev, h, h_next], axis=0)
    n_ext = ts + 2 * halo
    chunk = d_ff // FF_CHUNKS
    ffn = jnp.zeros((ts, h.shape[1]), F32)
    for c in range(FF_CHUNKS):
        lo = c * chunk
        a = jnp.dot(h_ext, wup_ref[:, lo:lo + chunk], preferred_element_type=F32)
        b = jnp.dot(h, wup_ref[:, d_ff + lo:d_ff + lo + chunk], preferred_element_type=F32)
        before = pltpu.roll(a, 1, 0)[halo:halo + ts]
        after = pltpu.roll(a, n_ext - 1, 0)[halo:halo + ts]
        a = (before * cw_ref[0:1, lo:lo + chunk] + a[halo:halo + ts] * cw_ref[1:2, lo:lo + chunk]
             + after * cw_ref[2:3, lo:lo + chunk] + cb_ref[:, lo:lo + chunk])
        g = (a * jax.nn.sigmoid(a) * b).astype(BF16)
        ffn = ffn + jnp.dot(g, wdn_ref[lo:lo + chunk, :], preferred_element_type=F32)
    x2 = x1_ref[...] + mod_ref[0, 5:6, :] * ffn
    o_ref[...] = _rms(x2, gf_ref[...], EPS)


def _conv_ffn(h2, x1, mod3, w_up, conv_w, conv_b, w_down, final_g, seq):
    rows, d = x1.shape
    d_ff = w_down.shape[0]
    ts = ROW_TILE
    nb = seq // ts
    assert d_ff % (FF_CHUNKS * LANES) == 0
    row_blk = pl.BlockSpec((ts, d), lambda i: (i, 0))
    prev, nxt = _halo_specs(ts, d, rows, BF16_SUBLANES)
    return pl.pallas_call(
        functools.partial(_ffn_kernel, d_ff=d_ff, tiles_per_seq=nb),
        grid=(rows // ts,),
        in_specs=[row_blk, prev, nxt, row_blk,
                  pl.BlockSpec((1, 6, d), lambda i: (i // nb, 0, 0)),
                  _resident((d, 2 * d_ff)), _resident((3, d_ff)), _resident((1, d_ff)),
                  _resident((d_ff, d)), _resident((1, d))],
        out_specs=row_blk,
        out_shape=jax.ShapeDtypeStruct((rows, d), F32),
        compiler_params=_params("parallel"),
        name="conv_ffn",
    )(h2, h2, h2, x1, mod3, w_up, conv_w, conv_b, w_down, final_g)


def kernel(x, c, positions, w_ada, b_ada, norm1_g, w_in, conv_w, lambda_q1, lambda_k1, lambda_q2,
           lambda_k2, subln_g, w_attn_o, w_conv_o, w_gate, b_gate, w_out, norm2_g, w_up,
           ffn_conv_w, ffn_conv_b, w_down, final_g):
    batch, seq, d = x.shape
    depth = w_in.shape[0]
    rows = batch * seq
    cos128, sin128 = _rope_tables(positions)
    xf = x.reshape(rows, d)
    for layer in range(depth):
        lambda_init = 0.8 - 0.6 * math.exp(-0.3 * layer)
        mod3 = _ada_mod(c, w_ada[layer], b_ada[layer]).reshape(batch, 6, d)
        q, k, v, gb, gcu, gates = _in_proj(
            xf, mod3, norm1_g[layer].reshape(1, d), w_in[layer].astype(BF16),
            w_gate[layer].astype(BF16), b_gate[layer].reshape(1, 2 * d), cos128, sin128, seq)
        attn = _diff_attn(q, k, v,
                          lambda_q1[layer].reshape(1, HEAD_DIM), lambda_k1[layer].reshape(1, HEAD_DIM),
                          lambda_q2[layer].reshape(1, HEAD_DIM), lambda_k2[layer].reshape(1, HEAD_DIM),
                          subln_g[layer].reshape(1, HEAD_WIDTH), batch, seq, lambda_init)
        x1, h2 = _mix_out(attn, gb, gcu, gates, xf, mod3, conv_w[layer],
                          w_attn_o[layer].astype(BF16), w_conv_o[layer].astype(BF16),
                          w_out[layer].astype(BF16), norm2_g[layer].reshape(1, d), seq)
        last = layer == depth - 1
        assert last, "conv_ffn fuses the final RMSNorm: DEPTH > 1 needs an un-normalised variant"
        xf = _conv_ffn(h2, x1, mod3, w_up[layer].astype(BF16), ffn_conv_w[layer],
                       ffn_conv_b[layer].reshape(1, -1), w_down[layer].astype(BF16),
                       final_g.reshape(1, d), seq)
    return xf.reshape(batch, seq, d)
```

```python
import functools
import math

import jax
import jax.numpy as jnp
from jax import lax
from jax.experimental import pallas as pl
from jax.experimental.pallas import tpu as pltpu

F32 = jnp.float32
BF16 = jnp.bfloat16

HEAD_DIM = 64
HEAD_WIDTH = 2 * HEAD_DIM
ROPE_THETA = 10000.0
EPS = 1e-6
SUBLN_EPS = 1e-5
LANES = 128
BF16_SUBLANES = 16
VMEM_LIMIT = 56 * 1024 * 1024

ROW_TILE = 512
Q_TILE = 256
K_CHUNK = 512
FF_CHUNKS = 2
MIX_ROW_GROUPS = 2


def _params(*sem, **kw):
    return pltpu.CompilerParams(dimension_semantics=sem, vmem_limit_bytes=VMEM_LIMIT, **kw)


def _rms(x, g, eps):
    return x * lax.rsqrt(jnp.mean(x * x, axis=-1, keepdims=True) + eps) * g


def _resident(shape):
    return pl.BlockSpec(shape, lambda *_: (0,) * len(shape), pipeline_mode=pl.Buffered(1))


def _ada_kernel(c_ref, w_ref, b_ref, o_ref):
    c = c_ref[...]
    c_act = (c * jax.nn.sigmoid(c)).astype(BF16)
    o_ref[...] = jnp.dot(c_act, w_ref[...].astype(BF16), preferred_element_type=F32) + b_ref[...]


def _ada_mod(c, w_ada, b_ada):
    bsz, d = c.shape
    n = w_ada.shape[1]
    return pl.pallas_call(
        _ada_kernel,
        grid=(n // d,),
        in_specs=[pl.BlockSpec((bsz, d), lambda j: (0, 0)),
                  pl.BlockSpec((d, d), lambda j: (0, j)),
                  pl.BlockSpec((1, d), lambda j: (0, j))],
        out_specs=pl.BlockSpec((bsz, d), lambda j: (0, j)),
        out_shape=jax.ShapeDtypeStruct((bsz, n), F32),
        compiler_params=_params("parallel"),
        name="ada_mod",
    )(c, w_ada, b_ada.reshape(1, n))


def _rope(t, cos, sin_signed):
    lane = lax.broadcasted_iota(jnp.int32, (1, LANES), 1)
    first_half = (lane % HEAD_DIM) < (HEAD_DIM // 2)
    out = []
    for cb in range(t.shape[1] // LANES):
        blk = t[:, cb * LANES:(cb + 1) * LANES]
        upper = pltpu.roll(blk, LANES - HEAD_DIM // 2, 1)
        lower = pltpu.roll(blk, HEAD_DIM // 2, 1)
        partner = jnp.where(first_half, upper, lower)
        out.append(blk * cos + partner * sin_signed)
    return jnp.concatenate(out, axis=1)


def _rope_tables(pos_row, inv_col):
    ang = inv_col * pos_row.astype(F32)
    cos_t, sin_t = jnp.cos(ang), jnp.sin(ang)
    reps = LANES // HEAD_DIM
    cos = jnp.concatenate([cos_t, cos_t] * reps, axis=0).T
    sin = jnp.concatenate([-sin_t, sin_t] * reps, axis=0).T
    return cos, sin


def _in_proj_kernel(x_ref, pos_ref, inv_ref, mod_ref, g_ref, win_ref, wg_ref, bg_ref,
                    q_ref, k_ref, v_ref, gb_ref, gcu_ref, gate_ref, *, d):
    h = _rms(x_ref[...], g_ref[...] * (1.0 + mod_ref[0, 1:2, :]), EPS) + mod_ref[0, 0:1, :]
    h = h.astype(BF16)

    def proj(j):
        return jnp.dot(h, win_ref[:, j * d:(j + 1) * d], preferred_element_type=F32)

    cos, sin = _rope_tables(pos_ref[0], inv_ref[...])
    q_ref[...] = (_rope(proj(0), cos, sin) * (HEAD_DIM ** -0.5 * math.log2(math.e))).astype(BF16)
    k_ref[...] = _rope(proj(1), cos, sin).astype(BF16)
    v_ref[...] = proj(2).astype(BF16)
    gb_ref[...] = proj(3).astype(BF16)
    gcu_ref[...] = (proj(4) * proj(5)).astype(BF16)
    for j in range(2):
        z = jnp.dot(h, wg_ref[:, j * d:(j + 1) * d], preferred_element_type=F32)
        gate_ref[:, j * d:(j + 1) * d] = jax.nn.sigmoid(z + bg_ref[:, j * d:(j + 1) * d]).astype(BF16)


def _in_proj(x2, positions, mod3, norm_g, w_in, w_gate, b_gate, seq):
    rows, d = x2.shape
    ts = ROW_TILE
    nb = seq // ts
    assert w_in.shape[1] == 6 * d and w_gate.shape[1] == 2 * d and seq % ts == 0
    half = HEAD_DIM // 2
    inv_freq = ROPE_THETA ** (-jnp.arange(0, HEAD_DIM, 2, dtype=F32) / HEAD_DIM)
    row_blk = lambda w: pl.BlockSpec((ts, w), lambda i: (i, 0))
    return pl.pallas_call(
        functools.partial(_in_proj_kernel, d=d),
        grid=(rows // ts,),
        in_specs=[row_blk(d),
                  pl.BlockSpec((1, 1, ts), lambda i: (i, 0, 0)),
                  _resident((half, 1)),
                  pl.BlockSpec((1, 6, d), lambda i: (i // nb, 0, 0)),
                  _resident((1, d)), _resident((d, 6 * d)), _resident((d, 2 * d)),
                  _resident((1, 2 * d))],
        out_specs=[row_blk(d)] * 5 + [row_blk(2 * d)],
        out_shape=[jax.ShapeDtypeStruct((rows, d), BF16)] * 5
                  + [jax.ShapeDtypeStruct((rows, 2 * d), BF16)],
        compiler_params=_params("parallel"),
        name="in_proj",
    )(x2, positions.reshape(rows // ts, 1, ts), inv_freq.reshape(half, 1), mod3, norm_g,
      w_in, w_gate, b_gate)


def _attn_kernel(q_ref, k_ref, v_ref, lq1_ref, lk1_ref, lq2_ref, lk2_ref, g_ref, o_ref,
                 vt_ref, s0_ref, s1_ref, acc_ref, l_ref, *, lambda_init, bq, kc):
    seq = q_ref.shape[0]
    nq, nk = seq // bq, seq // kc
    nt_dims = (((1,), (1,)), ((), ()))
    s_refs = (s0_ref, s1_ref)
    lam =(jnp.exp(jnp.sum(lq1_ref[...] * lk1_ref[...], axis=-1, keepdims=True))
           - jnp.exp(jnp.sum(lq2_ref[...] * lk2_ref[...], axis=-1, keepdims=True))
           + lambda_init)
    for c in range(nk):
        vt_ref[c] = v_ref[c * kc:(c + 1) * kc, :].T
    lane = lax.broadcasted_iota(jnp.int32, (1, HEAD_WIDTH), 1)
    m_init = jnp.full((1, 2 * bq), -jnp.inf, F32)
    l_init = jnp.zeros((1, 2 * bq), F32)

    def masked_queries(t):
        q = q_ref[pl.ds(pl.multiple_of(t * bq, bq), bq), :]
        zero = jnp.zeros_like(q)
        return jnp.concatenate([jnp.where(lane < HEAD_DIM, q, zero),
                                jnp.where(lane >= HEAD_DIM, q, zero)], axis=0)

    def finish(t):
        r = 1.0 / l_ref[...]
        acc = acc_ref[...]
        o_t = acc[:, :bq] * r[:, :bq] - acc[:, bq:] * (r[:, bq:] * lam)
        o = _rms(o_t.T, g_ref[...], SUBLN_EPS) * (1.0 - lambda_init)
        o_ref[pl.ds(pl.multiple_of(t * bq, bq), bq), :] = o.astype(BF16)

    def tile_step(score_tile, score_slot, value_tile, finish_tile, m_prev):
        if finish_tile is not None:
            finish(finish_tile)
        qq = None if score_tile is None else masked_queries(score_tile)
        m, l, acc = m_init, l_init, None
        for c in range(nk):
            if score_tile is not None:
                s = lax.dot_general(k_ref[c * kc:(c + 1) * kc, :], qq, nt_dims,
                                    preferred_element_type=F32)
                s_refs[score_slot][c] = s
                m = jnp.maximum(m, jnp.max(s, axis=0, keepdims=True))
            if value_tile is not None:
                e = jnp.exp2(s_refs[1 - score_slot][c] - m_prev)
                pv = jnp.dot(vt_ref[c], e.astype(BF16), preferred_element_type=F32)
                acc = pv if acc is None else acc + pv
                l = l + jnp.sum(e, axis=0, keepdims=True)
        if value_tile is not None:
            acc_ref[...] = acc
            l_ref[...] = l
        return m

    def tile_pair(u, m_prev):
        m_even = tile_step(2 * u + 2, 0, 2 * u + 1, 2 * u, m_prev)
        return tile_step(2 * u + 3, 1, 2 * u + 2, 2 * u + 1, m_even)

    assert nq % 2 == 0 and nq >= 4
    m_prev = tile_step(0, 0, None, None, None)
    m_prev = tile_step(1, 1, 0, None, m_prev)
    m_prev = lax.fori_loop(0, nq // 2 - 1, tile_pair, m_prev)
    tile_step(None, 0, nq - 1, nq - 2, m_prev)
    finish(nq - 1)


def _diff_attn(q, k, v, lq1, lk1, lq2, lk2, subln_g, batch, seq, lambda_init):
    rows, width = q.shape
    heads = width // HEAD_WIDTH
    bq, kc = Q_TILE, K_CHUNK
    assert seq % bq == 0 and seq % kc == 0
    lam_spec = pl.BlockSpec((1, HEAD_DIM), lambda b, h: (0, 0))
    head_spec = pl.BlockSpec((seq, HEAD_WIDTH), lambda b, h: (b, h))
    return pl.pallas_call(
        functools.partial(_attn_kernel, lambda_init=lambda_init, bq=bq, kc=kc),
        grid=(batch, heads),
        in_specs=[head_spec, head_spec, head_spec, lam_spec, lam_spec, lam_spec, lam_spec,
                  pl.BlockSpec((1, HEAD_WIDTH), lambda b, h: (0, 0))],
        out_specs=head_spec,
        out_shape=jax.ShapeDtypeStruct((rows, width), BF16),
        scratch_shapes=[pltpu.VMEM((seq // kc, HEAD_WIDTH, kc), BF16),
                        pltpu.VMEM((seq // kc, kc, 2 * bq), F32),
                        pltpu.VMEM((seq // kc, kc, 2 * bq), F32),
                        pltpu.VMEM((HEAD_WIDTH, 2 * bq), F32),
                        pltpu.VMEM((1, 2 * bq), F32)],
        compiler_params=_params("parallel", "parallel"),
        name="diff_attn",
    )(q, k, v, lq1, lk1, lq2, lk2, subln_g)


def _conv3(cur, prev_row, next_row, w_ref):
    n = cur.shape[0]
    edge = 8
    row = lax.broadcasted_iota(jnp.int32, (edge, 1), 0)
    before = pltpu.roll(cur, 1, 0)
    after = pltpu.roll(cur, n - 1, 0)
    before = jnp.concatenate([jnp.where(row == 0, prev_row, before[:edge]), before[edge:]], axis=0)
    after = jnp.concatenate([after[:n - edge],
                             jnp.where(row == edge - 1, next_row, after[n - edge:])], axis=0)
    return before * w_ref[0:1, :] + cur * w_ref[1:2, :] + after * w_ref[2:3, :]


def _mix_kernel(attn_ref, gb_ref, gcu_ref, gcu_prev_ref, gcu_next_ref, gate_ref, x_ref, mod_ref,
                cw_ref, wa_ref, wc_ref, wo_ref, g2_ref, x1_ref, h2_ref, *, d, tiles_per_seq):
    t = pl.program_id(0) % tiles_per_seq
    halo = gcu_prev_ref.shape[0]
    prev_row = gcu_prev_ref[...].astype(F32)[halo - 1:halo, :]
    next_row = gcu_next_ref[...].astype(F32)[0:1, :]
    prev_row = jnp.where(t == 0, 0.0, prev_row)
    next_row = jnp.where(t == tiles_per_seq - 1, 0.0, next_row)
    conv = _conv3(gcu_ref[...].astype(F32), prev_row, next_row, cw_ref)
    y_b_in = (gb_ref[...].astype(F32) * conv).astype(BF16)
    rows = attn_ref.shape[0] // MIX_ROW_GROUPS
    gain2 = g2_ref[...] * (1.0 + mod_ref[0, 4:5, :])
    for r in range(MIX_ROW_GROUPS):
        sl = slice(r * rows, (r + 1) * rows)
        y_a = jnp.dot(attn_ref[sl, :], wa_ref[...], preferred_element_type=F32)
        y_b = jnp.dot(y_b_in[sl, :], wc_ref[...], preferred_element_type=F32)
        merged = gate_ref[sl, :d] * y_a.astype(BF16) + gate_ref[sl, d:] * y_b.astype(BF16)
        mix = jnp.dot(merged, wo_ref[...], preferred_element_type=F32)
        x1 = x_ref[sl, :] + mod_ref[0, 2:3, :] * mix
        x1_ref[sl, :] = x1
        h2_ref[sl, :] = (_rms(x1, gain2, EPS) + mod_ref[0, 3:4, :]).astype(BF16)


def _halo_specs(ts, width, rows, halo):
    per_tile = ts // halo
    last = rows // halo - 1
    prev = pl.BlockSpec((halo, width), lambda i: (jnp.maximum(i * per_tile - 1, 0), 0))
    nxt = pl.BlockSpec((halo, width), lambda i: (jnp.minimum((i + 1) * per_tile, last), 0))
    return prev, nxt


def _mix_out(attn, gb, gcu, gates, x2, mod3, conv_w, w_attn_o, w_conv_o, w_out, norm_g, seq):
    rows, d = x2.shape
    ts = ROW_TILE
    nb = seq // ts
    row_blk = lambda w: pl.BlockSpec((ts, w), lambda i: (i, 0))
    prev, nxt = _halo_specs(ts, d, rows, BF16_SUBLANES)
    return pl.pallas_call(
        functools.partial(_mix_kernel, d=d, tiles_per_seq=nb),
        grid=(rows // ts,),
        in_specs=[row_blk(d), row_blk(d), row_blk(d), prev, nxt, row_blk(2 * d), row_blk(d),
                  pl.BlockSpec((1, 6, d), lambda i: (i // nb, 0, 0)),
                  _resident((3, d)), _resident((d, d)), _resident((d, d)), _resident((d, d)),
                  _resident((1, d))],
        out_specs=[row_blk(d), row_blk(d)],
        out_shape=[jax.ShapeDtypeStruct((rows, d), F32), jax.ShapeDtypeStruct((rows, d), BF16)],
        compiler_params=_params("parallel"),
        name="mix_out",
    )(attn, gb, gcu, gcu, gcu, gates, x2, mod3, conv_w, w_attn_o, w_conv_o, w_out, norm_g)


def _ffn_kernel(h_ref, h_prev_ref, h_next_ref, x1_ref, mod_ref, wup_ref, cw_ref, cb_ref, wdn_ref,
                gf_ref, o_ref, *, d_ff, tiles_per_seq):
    t = pl.program_id(0) % tiles_per_seq
    ts = h_ref.shape[0]
    halo = h_prev_ref.shape[0]
    h = h_ref[...]
    h_prev = jnp.where(t == 0, jnp.zeros_like(h_prev_ref[...]), h_prev_ref[...])
    h_next = jnp.where(t == tiles_per_seq - 1, jnp.zeros_like(h_next_ref[...]), h_next_ref[...])
    h_ext = jnp.concatenate([h_prev, h, h_next], axis=0)
    n_ext = ts + 2 * halo
    chunk = d_ff // FF_CHUNKS
    ffn = jnp.zeros((ts, h.shape[1]), F32)
    for c in range(FF_CHUNKS):
        lo = c * chunk
        a = jnp.dot(h_ext, wup_ref[:, lo:lo + chunk], preferred_element_type=F32)
        b = jnp.dot(h, wup_ref[:, d_ff + lo:d_ff + lo + chunk], preferred_element_type=F32)
        before = pltpu.roll(a, 1, 0)[halo:halo + ts]
        after = pltpu.roll(a, n_ext - 1, 0)[halo:halo + ts]
        a = (before * cw_ref[0:1, lo:lo + chunk] + a[halo:halo + ts] * cw_ref[1:2, lo:lo + chunk]
             + after * cw_ref[2:3, lo:lo + chunk] + cb_ref[:, lo:lo + chunk])
        g = (a * jax.nn.sigmoid(a) * b).astype(BF16)
        ffn = ffn + jnp.dot(g, wdn_ref[lo:lo + chunk, :], preferred_element_type=F32)
    x2 = x1_ref[...] + mod_ref[0, 5:6, :] * ffn
    o_ref[...] = _rms(x2, gf_ref[...], EPS)


def _conv_ffn(h2, x1, mod3, w_up, conv_w, conv_b, w_down, final_g, seq):
    rows, d = x1.shape
    d_ff = w_down.shape[0]
    ts = ROW_TILE
    nb = seq // ts
    assert d_ff % (FF_CHUNKS * LANES) == 0
    row_blk = pl.BlockSpec((ts, d), lambda i: (i, 0))
    prev, nxt = _halo_specs(ts, d, rows, BF16_SUBLANES)
    return pl.pallas_call(
        functools.partial(_ffn_kernel, d_ff=d_ff, tiles_per_seq=nb),
        grid=(rows // ts,),
        in_specs=[row_blk, prev, nxt, row_blk,
                  pl.BlockSpec((1, 6, d), lambda i: (i // nb, 0, 0)),
                  _resident((d, 2 * d_ff)), _resident((3, d_ff)), _resident((1, d_ff)),
                  _resident((d_ff, d)), _resident((1, d))],
        out_specs=row_blk,
        out_shape=jax.ShapeDtypeStruct((rows, d), F32),
        compiler_params=_params("parallel"),
        name="conv_ffn",
    )(h2, h2, h2, x1, mod3, w_up, conv_w, conv_b, w_down, final_g)


def kernel(x, c, positions, w_ada, b_ada, norm1_g, w_in, conv_w, lambda_q1, lambda_k1, lambda_q2,
           lambda_k2, subln_g, w_attn_o, w_conv_o, w_gate, b_gate, w_out, norm2_g, w_up,
           ffn_conv_w, ffn_conv_b, w_down, final_g):
    batch, seq, d = x.shape
    depth = w_in.shape[0]
    rows = batch * seq
    xf = x.reshape(rows, d)
    for layer in range(depth):
        lambda_init = 0.8 - 0.6 * math.exp(-0.3 * layer)
        mod3 = _ada_mod(c, w_ada[layer], b_ada[layer]).reshape(batch, 6, d)
        q, k, v, gb, gcu, gates = _in_proj(
            xf, positions, mod3, norm1_g[layer].reshape(1, d), w_in[layer].astype(BF16),
            w_gate[layer].astype(BF16), b_gate[layer].reshape(1, 2 * d), seq)
        attn = _diff_attn(q, k, v,
                          lambda_q1[layer].reshape(1, HEAD_DIM), lambda_k1[layer].reshape(1, HEAD_DIM),
                          lambda_q2[layer].reshape(1, HEAD_DIM), lambda_k2[layer].reshape(1, HEAD_DIM),
                          subln_g[layer].reshape(1, HEAD_WIDTH), batch, seq, lambda_init)
        x1, h2 = _mix_out(attn, gb, gcu, gates, xf, mod3, conv_w[layer],
                          w_attn_o[layer].astype(BF16), w_conv_o[layer].astype(BF16),
                          w_out[layer].astype(BF16), norm2_g[layer].reshape(1, d), seq)
        last = layer == depth - 1
        assert last, "conv_ffn fuses the final RMSNorm: DEPTH > 1 needs an un-normalised variant"
        xf = _conv_ffn(h2, x1, mod3, w_up[layer].astype(BF16), ffn_conv_w[layer],
                       ffn_conv_b[layer].reshape(1, -1), w_down[layer].astype(BF16),
                       final_g.reshape(1, d), seq)
    return xf.reshape(batch, seq, d)
```

```python
import functools
import math

import jax
import jax.numpy as jnp
from jax import lax
from jax.experimental import pallas as pl
from jax.experimental.pallas import tpu as pltpu

F32 = jnp.float32
BF16 = jnp.bfloat16

HEAD_DIM = 64
HEAD_WIDTH = 2 * HEAD_DIM
ROPE_THETA = 10000.0
EPS = 1e-6
SUBLN_EPS = 1e-5
LANES = 128
BF16_SUBLANES = 16
MXU_TILE = 256
VMEM_LIMIT = 56 * 1024 * 1024

ROW_TILE = 512
Q_TILE = 256
HEADS_PER_STEP = 2
K_CHUNK = 512
FF_CHUNKS = 1
MIX_ROW_GROUPS = 2


def _params(*sem, **kw):
    return pltpu.CompilerParams(dimension_semantics=sem, vmem_limit_bytes=VMEM_LIMIT, **kw)


def _rms(x, g, eps):
    return x * lax.rsqrt(jnp.mean(x * x, axis=-1, keepdims=True) + eps) * g


def _resident(shape):
    return pl.BlockSpec(shape, lambda *_: (0,) * len(shape), pipeline_mode=pl.Buffered(1))


def _ada_kernel(c_ref, w_ref, b_ref, o_ref):
    c = c_ref[...]
    c_act = (c * jax.nn.sigmoid(c)).astype(BF16)
    o_ref[...] = jnp.dot(c_act, w_ref[...].astype(BF16), preferred_element_type=F32) + b_ref[...]


def _ada_mod(c, w_ada, b_ada):
    bsz, d = c.shape
    n = w_ada.shape[1]
    return pl.pallas_call(
        _ada_kernel,
        grid=(n // d,),
        in_specs=[pl.BlockSpec((bsz, d), lambda j: (0, 0)),
                  pl.BlockSpec((d, d), lambda j: (0, j)),
                  pl.BlockSpec((1, d), lambda j: (0, j))],
        out_specs=pl.BlockSpec((bsz, d), lambda j: (0, j)),
        out_shape=jax.ShapeDtypeStruct((bsz, n), F32),
        compiler_params=_params("parallel"),
        name="ada_mod",
    )(c, w_ada, b_ada.reshape(1, n))


def _rope(t, cos, sin_signed):
    lane = lax.broadcasted_iota(jnp.int32, (1, LANES), 1)
    first_half = (lane % HEAD_DIM) < (HEAD_DIM // 2)
    out = []
    for cb in range(t.shape[1] // LANES):
        blk = t[:, cb * LANES:(cb + 1) * LANES]
        upper = pltpu.roll(blk, LANES - HEAD_DIM // 2, 1)
        lower = pltpu.roll(blk, HEAD_DIM // 2, 1)
        partner = jnp.where(first_half, upper, lower)
        out.append(blk * cos + partner * sin_signed)
    return jnp.concatenate(out, axis=1)


def _rope_tables(pos_row, inv_col):
    ang = inv_col * pos_row.astype(F32)
    cos_t, sin_t = jnp.cos(ang), jnp.sin(ang)
    reps = LANES // HEAD_DIM
    cos = jnp.concatenate([cos_t, cos_t] * reps, axis=0).T
    sin = jnp.concatenate([-sin_t, sin_t] * reps, axis=0).T
    return cos, sin


def _in_proj_kernel(x_ref, pos_ref, inv_ref, mod_ref, g_ref, win_ref, wg_ref, bg_ref,
                    q_ref, k_ref, v_ref, gb_ref, gcu_ref, gate_ref, *, d):
    h = _rms(x_ref[...], g_ref[...] * (1.0 + mod_ref[0, 1:2, :]), EPS) + mod_ref[0, 0:1, :]
    h = h.astype(BF16)

    def proj(j):
        return jnp.dot(h, win_ref[:, j * d:(j + 1) * d], preferred_element_type=F32)

    cos, sin = _rope_tables(pos_ref[0], inv_ref[...])
    q_ref[...] = (_rope(proj(0), cos, sin) * (HEAD_DIM ** -0.5 * math.log2(math.e))).astype(BF16)
    k_ref[...] = _rope(proj(1), cos, sin).astype(BF16)
    v_ref[...] = proj(2).astype(BF16)
    gb_ref[...] = proj(3).astype(BF16)
    gcu_ref[...] = (proj(4) * proj(5)).astype(BF16)
    for j in range(2):
        z = jnp.dot(h, wg_ref[:, j * d:(j + 1) * d], preferred_element_type=F32)
        gate_ref[:, j * d:(j + 1) * d] = jax.nn.sigmoid(z + bg_ref[:, j * d:(j + 1) * d]).astype(BF16)


def _in_proj(x2, positions, mod3, norm_g, w_in, w_gate, b_gate, seq):
    rows, d = x2.shape
    ts = ROW_TILE
    nb = seq // ts
    assert w_in.shape[1] == 6 * d and w_gate.shape[1] == 2 * d and seq % ts == 0
    half = HEAD_DIM // 2
    inv_freq = ROPE_THETA ** (-jnp.arange(0, HEAD_DIM, 2, dtype=F32) / HEAD_DIM)
    row_blk = lambda w: pl.BlockSpec((ts, w), lambda i: (i, 0))
    return pl.pallas_call(
        functools.partial(_in_proj_kernel, d=d),
        grid=(rows // ts,),
        in_specs=[row_blk(d),
                  pl.BlockSpec((1, 1, ts), lambda i: (i, 0, 0)),
                  _resident((half, 1)),
                  pl.BlockSpec((1, 6, d), lambda i: (i // nb, 0, 0)),
                  _resident((1, d)), _resident((d, 6 * d)), _resident((d, 2 * d)),
                  _resident((1, 2 * d))],
        out_specs=[row_blk(d)] * 5 + [row_blk(2 * d)],
        out_shape=[jax.ShapeDtypeStruct((rows, d), BF16)] * 5
                  + [jax.ShapeDtypeStruct((rows, 2 * d), BF16)],
        compiler_params=_params("parallel"),
        name="in_proj",
    )(x2, positions.reshape(rows // ts, 1, ts), inv_freq.reshape(half, 1), mod3, norm_g,
      w_in, w_gate, b_gate)


def _attn_kernel(q_ref, k_ref, v_ref, lq1_ref, lk1_ref, lq2_ref, lk2_ref, g_ref, o_ref,
                 vt_ref, s0_ref, s1_ref, acc_ref, *, lambda_init, bq, kc):
    seq = q_ref.shape[0]
    n_heads = q_ref.shape[1] // HEAD_WIDTH
    nq, nk = seq // bq, seq // kc
    nt_dims = (((1,), (1,)), ((), ()))
    s_refs = (s0_ref, s1_ref)
    lam = (jnp.exp(jnp.sum(lq1_ref[...] * lk1_ref[...], axis=-1, keepdims=True))
           - jnp.exp(jnp.sum(lq2_ref[...] * lk2_ref[...], axis=-1, keepdims=True))
           + lambda_init)
    for hh in range(n_heads):
        for c in range(nk):
            vt_ref[hh, c, :HEAD_WIDTH, :] = v_ref[c * kc:(c + 1) * kc, _head_cols(hh)].T
            vt_ref[hh, c, HEAD_WIDTH:, :] = jnp.ones((BF16_SUBLANES, kc), BF16)
    lane = lax.broadcasted_iota(jnp.int32, (1, HEAD_WIDTH), 1)
    m_init = jnp.full((8, 2 * bq), -jnp.inf, F32)

    def tile_rows(t):
        return pl.ds(pl.multiple_of(t * bq, bq), bq)

    def masked_queries(hh, t):
        q = q_ref[tile_rows(t), _head_cols(hh)]
        zero = jnp.zeros_like(q)
        return jnp.concatenate([jnp.where(lane < HEAD_DIM, q, zero),
                                jnp.where(lane >= HEAD_DIM, q, zero)], axis=0)

    def finish(hh, t):
        r = 1.0 / acc_ref[HEAD_WIDTH:HEAD_WIDTH + 1, :]
        acc = acc_ref[:HEAD_WIDTH, :]
        o_t = acc[:, :bq] * r[:, :bq] - acc[:, bq:] * (r[:, bq:] * lam)
        o = _rms(o_t.T, g_ref[...], SUBLN_EPS) * (1.0 - lambda_init)
        o_ref[tile_rows(t), _head_cols(hh)] = o.astype(BF16)

    def tile_step(score, value, fin, score_slot, m_prev):
        if fin is not None:
            finish(*fin)
        qq = None if score is None else masked_queries(*score)
        m, acc = m_init, None
        for c in range(nk):
            if score is not None:
                s = lax.dot_general(k_ref[c * kc:(c + 1) * kc, _head_cols(score[0])], qq, nt_dims,
                                    preferred_element_type=F32)
                s_refs[score_slot][c] = s
                m = jnp.maximum(m, jnp.max(s.reshape(kc // 8, 8, 2 * bq), axis=0))
            if value is not None:
                e = jnp.exp2(s_refs[1 - score_slot][c] - m_prev)
                pv = jnp.dot(vt_ref[value[0], c], e.astype(BF16), preferred_element_type=F32)
                acc = pv if acc is None else acc + pv
        if value is not None:
            acc_ref[...] = acc
        return jnp.max(m, axis=0, keepdims=True)

    assert nq % 2 == 0 and nq >= 4
    m_prev = tile_step((0, 0), None, None, 0, None)
    m_prev = tile_step((0, 1), (0, 0), None, 1, m_prev)
    for hh in range(n_heads):
        def tile_pair(u, m_prev, hh=hh):
            m_even = tile_step((hh, 2 * u + 2), (hh, 2 * u + 1), (hh, 2 * u), 0, m_prev)
            return tile_step((hh, 2 * u + 3), (hh, 2 * u + 2), (hh, 2 * u + 1), 1, m_even)

        m_prev = lax.fori_loop(0, nq // 2 - 1, tile_pair, m_prev)
        if hh + 1 < n_heads:
            m_prev = tile_step((hh + 1, 0), (hh, nq - 1), (hh, nq - 2), 0, m_prev)
            m_prev = tile_step((hh + 1, 1), (hh + 1, 0), (hh, nq - 1), 1, m_prev)
        else:
            tile_step(None, (hh, nq - 1), (hh, nq - 2), 0, m_prev)
            finish(hh, nq - 1)


def _head_cols(hh):
    return slice(hh * HEAD_WIDTH, (hh + 1) * HEAD_WIDTH)


def _diff_attn(q, k, v, lq1, lk1, lq2, lk2, subln_g, batch, seq, lambda_init):
    rows, width = q.shape
    heads = width // HEAD_WIDTH
    bq, kc, hp = Q_TILE, K_CHUNK, HEADS_PER_STEP
    assert seq % bq == 0 and seq % kc == 0 and heads % hp == 0
    lam_spec = pl.BlockSpec((1, HEAD_DIM), lambda b, h: (0, 0))
    head_spec = pl.BlockSpec((seq, hp * HEAD_WIDTH), lambda b, h: (b, h))
    return pl.pallas_call(
        functools.partial(_attn_kernel, lambda_init=lambda_init, bq=bq, kc=kc),
        grid=(batch, heads // hp),
        in_specs=[head_spec, head_spec, head_spec, lam_spec, lam_spec, lam_spec, lam_spec,
                  pl.BlockSpec((1, HEAD_WIDTH), lambda b, h: (0, 0))],
        out_specs=head_spec,
        out_shape=jax.ShapeDtypeStruct((rows, width), BF16),
        scratch_shapes=[pltpu.VMEM((hp, seq // kc, HEAD_WIDTH + BF16_SUBLANES, kc), BF16),
                        pltpu.VMEM((seq // kc, kc, 2 * bq), F32),
                        pltpu.VMEM((seq // kc, kc, 2 * bq), F32),
                        pltpu.VMEM((HEAD_WIDTH + BF16_SUBLANES, 2 * bq), F32)],
        compiler_params=_params("parallel", "parallel"),
        name="diff_attn",
    )(q, k, v, lq1, lk1, lq2, lk2, subln_g)


def _conv3(cur, prev_row, next_row, w_ref):
    n = cur.shape[0]
    edge = 8
    row = lax.broadcasted_iota(jnp.int32, (edge, 1), 0)
    before = pltpu.roll(cur, 1, 0)
    after = pltpu.roll(cur, n - 1, 0)
    before = jnp.concatenate([jnp.where(row == 0, prev_row, before[:edge]), before[edge:]], axis=0)
    after = jnp.concatenate([after[:n - edge],
                             jnp.where(row == edge - 1, next_row, after[n - edge:])], axis=0)
    return before * w_ref[0:1, :] + cur * w_ref[1:2, :] + after * w_ref[2:3, :]


def _mix_kernel(attn_ref, gb_ref, gcu_ref, gcu_prev_ref, gcu_next_ref, gate_ref, x_ref, mod_ref,
                cw_ref, wa_ref, wc_ref, wo_ref, g2_ref, x1_ref, h2_ref, *, d, tiles_per_seq):
    t = pl.program_id(0) % tiles_per_seq
    halo = gcu_prev_ref.shape[0]
    prev_row = gcu_prev_ref[...].astype(F32)[halo - 1:halo, :]
    next_row = gcu_next_ref[...].astype(F32)[0:1, :]
    prev_row = jnp.where(t == 0, 0.0, prev_row)
    next_row = jnp.where(t == tiles_per_seq - 1, 0.0, next_row)
    conv = _conv3(gcu_ref[...].astype(F32), prev_row, next_row, cw_ref)
    y_b_in = (gb_ref[...].astype(F32) * conv).astype(BF16)
    rows = attn_ref.shape[0] // MIX_ROW_GROUPS
    gain2 = g2_ref[...] * (1.0 + mod_ref[0, 4:5, :])
    for r in range(MIX_ROW_GROUPS):
        sl = slice(r * rows, (r + 1) * rows)
        y_a = jnp.dot(attn_ref[sl, :], wa_ref[...], preferred_element_type=F32)
        y_b = jnp.dot(y_b_in[sl, :], wc_ref[...], preferred_element_type=F32)
        merged = gate_ref[sl, :d] * y_a.astype(BF16) + gate_ref[sl, d:] * y_b.astype(BF16)
        mix = jnp.dot(merged, wo_ref[...], preferred_element_type=F32)
        x1 = x_ref[sl, :] + mod_ref[0, 2:3, :] * mix
        x1_ref[sl, :] = x1
        h2_ref[sl, :] = (_rms(x1, gain2, EPS) + mod_ref[0, 3:4, :]).astype(BF16)


def _halo_specs(ts, width, rows, halo):
    per_tile = ts // halo
    last = rows // halo - 1
    prev = pl.BlockSpec((halo, width), lambda i: (jnp.maximum(i * per_tile - 1, 0), 0))
    nxt = pl.BlockSpec((halo, width), lambda i: (jnp.minimum((i + 1) * per_tile, last), 0))
    return prev, nxt


def _mix_out(attn, gb, gcu, gates, x2, mod3, conv_w, w_attn_o, w_conv_o, w_out, norm_g, seq):
    rows, d = x2.shape
    ts = ROW_TILE
    nb = seq // ts
    row_blk = lambda w: pl.BlockSpec((ts, w), lambda i: (i, 0))
    prev, nxt = _halo_specs(ts, d, rows, BF16_SUBLANES)
    return pl.pallas_call(
        functools.partial(_mix_kernel, d=d, tiles_per_seq=nb),
        grid=(rows // ts,),
        in_specs=[row_blk(d), row_blk(d), row_blk(d), prev, nxt, row_blk(2 * d), row_blk(d),
                  pl.BlockSpec((1, 6, d), lambda i: (i // nb, 0, 0)),
                  _resident((3, d)), _resident((d, d)), _resident((d, d)), _resident((d, d)),
                  _resident((1, d))],
        out_specs=[row_blk(d), row_blk(d)],
        out_shape=[jax.ShapeDtypeStruct((rows, d), F32), jax.ShapeDtypeStruct((rows, d), BF16)],
        compiler_params=_params("parallel"),
        name="mix_out",
    )(attn, gb, gcu, gcu, gcu, gates, x2, mod3, conv_w, w_attn_o, w_conv_o, w_out, norm_g)


def _ff_chunks(d_ff):
    tiles = d_ff // MXU_TILE
    assert tiles * MXU_TILE == d_ff and tiles >= FF_CHUNKS
    cuts = [MXU_TILE * ((c * tiles + FF_CHUNKS - 1) // FF_CHUNKS) for c in range(FF_CHUNKS)] + [d_ff]
    return list(zip(cuts[:-1], cuts[1:]))


def _ffn_kernel(h_ref, h_prev_ref, h_next_ref, x1_ref, mod_ref, wup_ref, cw_ref, cb_ref, wdn_ref,
                gf_ref, o_ref, *, d_ff, tiles_per_seq):
    t = pl.program_id(0) % tiles_per_seq
    ts = h_ref.shape[0]
    halo = h_prev_ref.shape[0]
    h = h_ref[...]
    h_prev = jnp.where(t == 0, jnp.zeros_like(h_prev_ref[...]), h_prev_ref[...])
    h_next = jnp.where(t == tiles_per_seq - 1, jnp.zeros_like(h_next_ref[...]), h_next_ref[...])
    h_ext = jnp.concatenate([h_prev, h, h_next], axis=0)
    n_ext = ts + 2 * halo
    ffn = None
    for lo, hi in _ff_chunks(d_ff):
        a = jnp.dot(h_ext, wup_ref[:, lo:hi], preferred_element_type=F32)
        b = jnp.dot(h, wup_ref[:, d_ff + lo:d_ff + hi], preferred_element_type=F32)
        before = pltpu.roll(a, 1, 0)[halo:halo + ts]
        after = pltpu.roll(a, n_ext - 1, 0)[halo:halo + ts]
        a = (before * cw_ref[0:1, lo:hi] + a[halo:halo + ts] * cw_ref[1:2, lo:hi]
             + after * cw_ref[2:3, lo:hi] + cb_ref[:, lo:hi])
        g = (a * jax.nn.sigmoid(a) * b).astype(BF16)
        part = jnp.dot(g, wdn_ref[lo:hi, :], preferred_element_type=F32)
        ffn = part if ffn is None else ffn + part
    x2 = x1_ref[...] + mod_ref[0, 5:6, :] * ffn
    o_ref[...] = _rms(x2, gf_ref[...], EPS)


def _conv_ffn(h2, x1, mod3, w_up, conv_w, conv_b, w_down, final_g, seq):
    rows, d = x1.shape
    d_ff = w_down.shape[0]
    ts = ROW_TILE
    nb = seq // ts
    row_blk = pl.BlockSpec((ts, d), lambda i: (i, 0))
    prev, nxt = _halo_specs(ts, d, rows, BF16_SUBLANES)
    return pl.pallas_call(
        functools.partial(_ffn_kernel, d_ff=d_ff, tiles_per_seq=nb),
        grid=(rows // ts,),
        in_specs=[row_blk, prev, nxt, row_blk,
                  pl.BlockSpec((1, 6, d), lambda i: (i // nb, 0, 0)),
                  _resident((d, 2 * d_ff)), _resident((3, d_ff)), _resident((1, d_ff)),
                  _resident((d_ff, d)), _resident((1, d))],
        out_specs=row_blk,
        out_shape=jax.ShapeDtypeStruct((rows, d), F32),
        compiler_params=_params("parallel"),
        name="conv_ffn",
    )(h2, h2, h2, x1, mod3, w_up, conv_w, conv_b, w_down, final_g)


def kernel(x, c, positions, w_ada, b_ada, norm1_g, w_in, conv_w, lambda_q1, lambda_k1, lambda_q2,
           lambda_k2, subln_g, w_attn_o, w_conv_o, w_gate, b_gate, w_out, norm2_g, w_up,
           ffn_conv_w, ffn_conv_b, w_down, final_g):
    batch, seq, d = x.shape
    depth = w_in.shape[0]
    rows = batch * seq
    xf = x.reshape(rows, d)
    for layer in range(depth):
        lambda_init = 0.8 - 0.6 * math.exp(-0.3 * layer)
        mod3 = _ada_mod(c, w_ada[layer], b_ada[layer]).reshape(batch, 6, d)
        q, k, v, gb, gcu, gates = _in_proj(
            xf, positions, mod3, norm1_g[layer].reshape(1, d), w_in[layer].astype(BF16),
            w_gate[layer].astype(BF16), b_gate[layer].reshape(1, 2 * d), seq)
        attn = _diff_attn(q, k, v,
                          lambda_q1[layer].reshape(1, HEAD_DIM), lambda_k1[layer].reshape(1, HEAD_DIM),
                          lambda_q2[layer].reshape(1, HEAD_DIM), lambda_k2[layer].reshape(1, HEAD_DIM),
                          subln_g[layer].reshape(1, HEAD_WIDTH), batch, seq, lambda_init)
        x1, h2 = _mix_out(attn, gb, gcu, gates, xf, mod3, conv_w[layer],
                          w_attn_o[layer].astype(BF16), w_conv_o[layer].astype(BF16),
                          w_out[layer].astype(BF16), norm2_g[layer].reshape(1, d), seq)
        last = layer == depth - 1
        assert last, "conv_ffn fuses the final RMSNorm: DEPTH > 1 needs an un-normalised variant"
        xf = _conv_ffn(h2, x1, mod3, w_up[layer].astype(BF16), ffn_conv_w[layer],
                       ffn_conv_b[layer].reshape(1, -1), w_down[layer].astype(BF16),
                       final_g.reshape(1, d), seq)
    return xf.reshape(batch, seq, d)
```

```python
import functools
import math

import jax
import jax.numpy as jnp
from jax import lax
from jax.experimental import pallas as pl
from jax.experimental.pallas import tpu as pltpu

F32 = jnp.float32
BF16 = jnp.bfloat16

HEAD_DIM = 64
HEAD_WIDTH = 2 * HEAD_DIM
ROPE_THETA = 10000.0
EPS = 1e-6
SUBLN_EPS = 1e-5
LANES = 128
F32_SUBLANES = 8
BF16_SUBLANES = 16
MXU_TILE = 256
VMEM_LIMIT = 56 * 1024 * 1024

ROW_TILE = 512
Q_TILE = 256
HEADS_PER_STEP = 2
K_CHUNK = 512
FF_CHUNKS = 1
IN_ROW_GROUPS = 2
MIX_ROW_TILE = 1024
FFN_ROW_TILE = 1024
MIX_ROW_GROUPS = 2


def _params(*sem):
    return pltpu.CompilerParams(dimension_semantics=sem, vmem_limit_bytes=VMEM_LIMIT)


def _rms(x, g, eps):
    return x * lax.rsqrt(jnp.mean(x * x, axis=-1, keepdims=True) + eps) * g


def _resident(shape):
    return pl.BlockSpec(shape, lambda *_: (0,) * len(shape), pipeline_mode=pl.Buffered(1))


def _halo_specs(ts, width, rows, halo):
    per_tile = ts // halo
    last = rows // halo - 1
    prev = pl.BlockSpec((halo, width), lambda i: (jnp.maximum(i * per_tile - 1, 0), 0))
    nxt = pl.BlockSpec((halo, width), lambda i: (jnp.minimum((i + 1) * per_tile, last), 0))
    return prev, nxt


def _conv3_rows(ext, halo, ts, w_ref, cols=slice(None)):
    n_ext = ts + 2 * halo
    before = pltpu.roll(ext, 1, 0)[halo:halo + ts]
    after = pltpu.roll(ext, n_ext - 1, 0)[halo:halo + ts]
    return (before * w_ref[0:1, cols] + ext[halo:halo + ts] * w_ref[1:2, cols]
            + after * w_ref[2:3, cols])


def _ada_kernel(c_ref, w_ref, b_ref, o_ref):
    c = c_ref[...]
    c_act = (c * jax.nn.sigmoid(c)).astype(BF16)
    o_ref[...] = jnp.dot(c_act, w_ref[...].astype(BF16), preferred_element_type=F32) + b_ref[...]


def _ada_mod(c, w_ada, b_ada):
    bsz, d = c.shape
    n = w_ada.shape[1]
    return pl.pallas_call(
        _ada_kernel,
        grid=(n // d,),
        in_specs=[pl.BlockSpec((bsz, d), lambda j: (0, 0)),
                  pl.BlockSpec((d, d), lambda j: (0, j)),
                  pl.BlockSpec((1, d), lambda j: (0, j))],
        out_specs=pl.BlockSpec((bsz, d), lambda j: (0, j)),
        out_shape=jax.ShapeDtypeStruct((bsz, n), F32),
        compiler_params=_params("parallel"),
        name="ada_mod",
    )(c, w_ada, b_ada.reshape(1, n))


def _rope(t, cos, sin_signed):
    lane = lax.broadcasted_iota(jnp.int32, (1, LANES), 1)
    first_half = (lane % HEAD_DIM) < (HEAD_DIM // 2)
    out = []
    for cb in range(t.shape[1] // LANES):
        blk = t[:, cb * LANES:(cb + 1) * LANES]
        upper = pltpu.roll(blk, LANES - HEAD_DIM // 2, 1)
        lower = pltpu.roll(blk, HEAD_DIM // 2, 1)
        partner = jnp.where(first_half, upper, lower)
        out.append(blk * cos + partner * sin_signed)
    return jnp.concatenate(out, axis=1)


def _rope_tables(pos_row, inv_col):
    ang = inv_col * pos_row.astype(F32)
    cos_t, sin_t = jnp.cos(ang), jnp.sin(ang)
    reps = LANES // HEAD_DIM
    cos = jnp.concatenate([cos_t, cos_t] * reps, axis=0).T
    sin = jnp.concatenate([-sin_t, sin_t] * reps, axis=0).T
    return cos, sin


def _in_proj_kernel(x_ref, x_prev_ref, x_next_ref, pos_ref, inv_ref, mod_ref, g_ref, win_ref, wg_ref,
                    bg_ref, cw_ref, q_ref, k_ref, v_ref, yb_ref, gate_ref, *, d, tiles_per_seq):
    gain = g_ref[...] * (1.0 + mod_ref[0, 1:2, :])

    def normed(x):
        return _rms(x, gain, EPS) + mod_ref[0, 0:1, :]

    def proj(h, j):
        return jnp.dot(h, win_ref[:, j * d:(j + 1) * d], preferred_element_type=F32)

    cos_all, sin_all = _rope_tables(pos_ref[0], inv_ref[...])
    ts = x_ref.shape[0]
    rows = ts // IN_ROW_GROUPS
    groups = [slice(r * rows, (r + 1) * rows) for r in range(IN_ROW_GROUPS)]
    h_rows = [normed(x_ref[sl, :]) for sl in groups]

    t = pl.program_id(0) % tiles_per_seq
    halo = x_prev_ref.shape[0]
    h_prev = jnp.where(t == 0, 0.0, normed(x_prev_ref[...]))
    h_next = jnp.where(t == tiles_per_seq - 1, 0.0, normed(x_next_ref[...]))
    h_ext = jnp.concatenate([h_prev] + h_rows + [h_next], axis=0).astype(BF16)
    conv = _conv3_rows(proj(h_ext, 4) * proj(h_ext, 5), halo, ts, cw_ref)

    for sl, h32 in zip(groups, h_rows):
        h = h32.astype(BF16)
        cos, sin = cos_all[sl, :], sin_all[sl, :]
        q_ref[sl, :] = (_rope(proj(h, 0), cos, sin) * (HEAD_DIM ** -0.5 * math.log2(math.e))).astype(BF16)
        k_ref[sl, :] = _rope(proj(h, 1), cos, sin).astype(BF16)
        v_ref[sl, :] = proj(h, 2).astype(BF16)
        yb_ref[sl, :] = (proj(h, 3) * conv[sl, :]).astype(BF16)
        for j in range(2):
            cols = slice(j * d, (j + 1) * d)
            z = jnp.dot(h, wg_ref[:, cols], preferred_element_type=F32)
            gate_ref[sl, cols] = jax.nn.sigmoid(z + bg_ref[:, cols]).astype(BF16)


def _in_proj(x2, positions, mod3, norm_g, w_in, w_gate, b_gate, conv_w, seq):
    rows, d = x2.shape
    ts = ROW_TILE
    nb = seq // ts
    assert w_in.shape[1] == 6 * d and w_gate.shape[1] == 2 * d and seq % ts == 0
    half = HEAD_DIM // 2
    inv_freq = ROPE_THETA ** (-jnp.arange(0, HEAD_DIM, 2, dtype=F32) / HEAD_DIM)
    row_blk = lambda w: pl.BlockSpec((ts, w), lambda i: (i, 0))
    prev, nxt = _halo_specs(ts, d, rows, F32_SUBLANES)
    return pl.pallas_call(
        functools.partial(_in_proj_kernel, d=d, tiles_per_seq=nb),
        grid=(rows // ts,),
        in_specs=[row_blk(d), prev, nxt,
                  pl.BlockSpec((1, 1, ts), lambda i: (i, 0, 0)),
                  _resident((half, 1)),
                  pl.BlockSpec((1, 6, d), lambda i: (i // nb, 0, 0)),
                  _resident((1, d)), _resident((d, 6 * d)), _resident((d, 2 * d)),
                  _resident((1, 2 * d)), _resident((3, d))],
        out_specs=[row_blk(d)] * 4 + [row_blk(2 * d)],
        out_shape=[jax.ShapeDtypeStruct((rows, d), BF16)] * 4
                  + [jax.ShapeDtypeStruct((rows, 2 * d), BF16)],
        compiler_params=_params("parallel"),
        name="in_proj",
    )(x2, x2, x2, positions.reshape(rows // ts, 1, ts), inv_freq.reshape(half, 1), mod3, norm_g,
      w_in, w_gate, b_gate, conv_w)


def _head_cols(hh):
    return slice(hh * HEAD_WIDTH, (hh + 1) * HEAD_WIDTH)


def _attn_kernel(q_ref, k_ref, v_ref, lq1_ref, lk1_ref, lq2_ref, lk2_ref, g_ref, o_ref,
                 vt_ref, s0_ref, s1_ref, acc_ref, *, lambda_init, bq, kc):
    seq = q_ref.shape[0]
    n_heads = q_ref.shape[1] // HEAD_WIDTH
    nq, nk = seq // bq, seq // kc
    nt_dims = (((1,), (1,)), ((), ()))
    s_refs = (s0_ref, s1_ref)
    lam = (jnp.exp(jnp.sum(lq1_ref[...] * lk1_ref[...], axis=-1, keepdims=True))
           - jnp.exp(jnp.sum(lq2_ref[...] * lk2_ref[...], axis=-1, keepdims=True))
           + lambda_init)
    for hh in range(n_heads):
        for c in range(nk):
            vt_ref[hh, c, :HEAD_WIDTH, :] = v_ref[c * kc:(c + 1) * kc, _head_cols(hh)].T
            vt_ref[hh, c, HEAD_WIDTH:, :] = jnp.ones((BF16_SUBLANES, kc), BF16)
    lane = lax.broadcasted_iota(jnp.int32, (1, HEAD_WIDTH), 1)
    m_init = jnp.full((F32_SUBLANES, 2 * bq), -jnp.inf, F32)

    def tile_rows(t):
        return pl.ds(pl.multiple_of(t * bq, bq), bq)

    def masked_queries(hh, t):
        q = q_ref[tile_rows(t), _head_cols(hh)]
        zero = jnp.zeros_like(q)
        return jnp.concatenate([jnp.where(lane < HEAD_DIM, q, zero),
                                jnp.where(lane >= HEAD_DIM, q, zero)], axis=0)

    def finish(hh, t):
        r = 1.0 / acc_ref[HEAD_WIDTH:HEAD_WIDTH + 1, :]
        acc = acc_ref[:HEAD_WIDTH, :]
        o_t = acc[:, :bq] * r[:, :bq] - acc[:, bq:] * (r[:, bq:] * lam)
        o = _rms(o_t.T, g_ref[...], SUBLN_EPS) * (1.0 - lambda_init)
        o_ref[tile_rows(t), _head_cols(hh)] = o.astype(BF16)

    def tile_step(score, value, fin, score_slot, m_prev):
        if fin is not None:
            finish(*fin)
        qq = None if score is None else masked_queries(*score)
        m, acc = m_init, None
        for c in range(nk):
            if score is not None:
                s = lax.dot_general(k_ref[c * kc:(c + 1) * kc, _head_cols(score[0])], qq, nt_dims,
                                    preferred_element_type=F32)
                s_refs[score_slot][c] = s
                m = jnp.maximum(m, jnp.max(s.reshape(kc // F32_SUBLANES, F32_SUBLANES, 2 * bq), axis=0))
            if value is not None:
                e = jnp.exp2(s_refs[1 - score_slot][c] - m_prev)
                pv = jnp.dot(vt_ref[value[0], c], e.astype(BF16), preferred_element_type=F32)
                acc = pv if acc is None else acc + pv
        if value is not None:
            acc_ref[...] = acc
        return jnp.max(m, axis=0, keepdims=True)

    assert nq % 2 == 0 and nq >= 4
    m_prev = tile_step((0, 0), None, None, 0, None)
    m_prev = tile_step((0, 1), (0, 0), None, 1, m_prev)
    for hh in range(n_heads):
        def tile_pair(u, m_prev, hh=hh):
            m_even = tile_step((hh, 2 * u + 2), (hh, 2 * u + 1), (hh, 2 * u), 0, m_prev)
            return tile_step((hh, 2 * u + 3), (hh, 2 * u + 2), (hh, 2 * u + 1), 1, m_even)

        m_prev = lax.fori_loop(0, nq // 2 - 1, tile_pair, m_prev)
        if hh + 1 < n_heads:
            m_prev = tile_step((hh + 1, 0), (hh, nq - 1), (hh, nq - 2), 0, m_prev)
            m_prev = tile_step((hh + 1, 1), (hh + 1, 0), (hh, nq - 1), 1, m_prev)
        else:
            tile_step(None, (hh, nq - 1), (hh, nq - 2), 0, m_prev)
            finish(hh, nq - 1)


def _diff_attn(q, k, v, lq1, lk1, lq2, lk2, subln_g, batch, seq, lambda_init):
    rows, width = q.shape
    heads = width // HEAD_WIDTH
    bq, kc, hp = Q_TILE, K_CHUNK, HEADS_PER_STEP
    assert seq % bq == 0 and seq % kc == 0 and heads % hp == 0
    lam_spec = pl.BlockSpec((1, HEAD_DIM), lambda b, h: (0, 0))
    head_spec = pl.BlockSpec((seq, hp * HEAD_WIDTH), lambda b, h: (b, h))
    return pl.pallas_call(
        functools.partial(_attn_kernel, lambda_init=lambda_init, bq=bq, kc=kc),
        grid=(batch, heads // hp),
        in_specs=[head_spec, head_spec, head_spec, lam_spec, lam_spec, lam_spec, lam_spec,
                  pl.BlockSpec((1, HEAD_WIDTH), lambda b, h: (0, 0))],
        out_specs=head_spec,
        out_shape=jax.ShapeDtypeStruct((rows, width), BF16),
        scratch_shapes=[pltpu.VMEM((hp, seq // kc, HEAD_WIDTH + BF16_SUBLANES, kc), BF16),
                        pltpu.VMEM((seq // kc, kc, 2 * bq), F32),
                        pltpu.VMEM((seq // kc, kc, 2 * bq), F32),
                        pltpu.VMEM((HEAD_WIDTH + BF16_SUBLANES, 2 * bq), F32)],
        compiler_params=_params("parallel", "parallel"),
        name="diff_attn",
    )(q, k, v, lq1, lk1, lq2, lk2, subln_g)


def _mix_kernel(attn_ref, yb_ref, gate_ref, x_ref, mod_ref, wa_ref, wc_ref, wo_ref, g2_ref,
                x1_ref, h2_ref, *, d):
    rows = attn_ref.shape[0] // MIX_ROW_GROUPS
    gain2 = g2_ref[...] * (1.0 + mod_ref[0, 4:5, :])
    for r in range(MIX_ROW_GROUPS):
        sl = slice(r * rows, (r + 1) * rows)
        y_a = jnp.dot(attn_ref[sl, :], wa_ref[...], preferred_element_type=F32)
        y_b = jnp.dot(yb_ref[sl, :], wc_ref[...], preferred_element_type=F32)
        merged = gate_ref[sl, :d] * y_a.astype(BF16) + gate_ref[sl, d:] * y_b.astype(BF16)
        mix = jnp.dot(merged, wo_ref[...], preferred_element_type=F32)
        x1 = x_ref[sl, :] + mod_ref[0, 2:3, :] * mix
        x1_ref[sl, :] = x1
        h2_ref[sl, :] = (_rms(x1, gain2, EPS) + mod_ref[0, 3:4, :]).astype(BF16)


def _mix_out(attn, yb, gates, x2, mod3, w_attn_o, w_conv_o, w_out, norm_g, seq):
    rows, d = x2.shape
    ts = MIX_ROW_TILE
    nb = seq // ts
    assert seq % ts == 0
    row_blk = lambda w: pl.BlockSpec((ts, w), lambda i: (i, 0))
    return pl.pallas_call(
        functools.partial(_mix_kernel, d=d),
        grid=(rows // ts,),
        in_specs=[row_blk(d), row_blk(d), row_blk(2 * d), row_blk(d),
                  pl.BlockSpec((1, 6, d), lambda i: (i // nb, 0, 0)),
                  _resident((d, d)), _resident((d, d)), _resident((d, d)), _resident((1, d))],
        out_specs=[row_blk(d), row_blk(d)],
        out_shape=[jax.ShapeDtypeStruct((rows, d), F32), jax.ShapeDtypeStruct((rows, d), BF16)],
        compiler_params=_params("parallel"),
        name="mix_out",
    )(attn, yb, gates, x2, mod3, w_attn_o, w_conv_o, w_out, norm_g)


def _ff_chunks(d_ff):
    tiles = d_ff // MXU_TILE
    assert tiles * MXU_TILE == d_ff and tiles >= FF_CHUNKS
    cuts = [MXU_TILE * ((c * tiles + FF_CHUNKS - 1) // FF_CHUNKS) for c in range(FF_CHUNKS)] + [d_ff]
    return list(zip(cuts[:-1], cuts[1:]))


def _ffn_kernel(h_ref, h_prev_ref, h_next_ref, x1_ref, mod_ref, wup_ref, cw_ref, cb_ref, wdn_ref,
                gf_ref, o_ref, *, d_ff, tiles_per_seq):
    t = pl.program_id(0) % tiles_per_seq
    ts = h_ref.shape[0]
    halo = h_prev_ref.shape[0]
    h = h_ref[...]
    h_prev = jnp.where(t == 0, jnp.zeros_like(h_prev_ref[...]), h_prev_ref[...])
    h_next = jnp.where(t == tiles_per_seq - 1, jnp.zeros_like(h_next_ref[...]), h_next_ref[...])
    h_ext = jnp.concatenate([h_prev, h, h_next], axis=0)
    ffn = None
    for lo, hi in _ff_chunks(d_ff):
        a_ext = jnp.dot(h_ext, wup_ref[:, lo:hi], preferred_element_type=F32)
        b = jnp.dot(h, wup_ref[:, d_ff + lo:d_ff + hi], preferred_element_type=F32)
        a = _conv3_rows(a_ext, halo, ts, cw_ref, slice(lo, hi)) + cb_ref[:, lo:hi]
        g = (a * jax.nn.sigmoid(a) * b).astype(BF16)
        part = jnp.dot(g, wdn_ref[lo:hi, :], preferred_element_type=F32)
        ffn = part if ffn is None else ffn + part
    x2 = x1_ref[...] + mod_ref[0, 5:6, :] * ffn
    o_ref[...] = _rms(x2, gf_ref[...], EPS)


def _conv_ffn(h2, x1, mod3, w_up, conv_w, conv_b, w_down, final_g, seq):
    rows, d = x1.shape
    d_ff = w_down.shape[0]
    ts = FFN_ROW_TILE
    nb = seq // ts
    assert seq % ts == 0
    row_blk = pl.BlockSpec((ts, d), lambda i: (i, 0))
    prev, nxt = _halo_specs(ts, d, rows, BF16_SUBLANES)
    return pl.pallas_call(
        functools.partial(_ffn_kernel, d_ff=d_ff, tiles_per_seq=nb),
        grid=(rows // ts,),
        in_specs=[row_blk, prev, nxt, row_blk,
                  pl.BlockSpec((1, 6, d), lambda i: (i // nb, 0, 0)),
                  _resident((d, 2 * d_ff)), _resident((3, d_ff)), _resident((1, d_ff)),
                  _resident((d_ff, d)), _resident((1, d))],
        out_specs=row_blk,
        out_shape=jax.ShapeDtypeStruct((rows, d), F32),
        compiler_params=_params("parallel"),
        name="conv_ffn",
    )(h2, h2, h2, x1, mod3, w_up, conv_w, conv_b, w_down, final_g)


def kernel(x, c, positions, w_ada, b_ada, norm1_g, w_in, conv_w, lambda_q1, lambda_k1, lambda_q2,
           lambda_k2, subln_g, w_attn_o, w_conv_o, w_gate, b_gate, w_out, norm2_g, w_up,
           ffn_conv_w, ffn_conv_b, w_down, final_g):
    batch, seq, d = x.shape
    depth = w_in.shape[0]
    rows = batch * seq
    xf = x.reshape(rows, d)
    for layer in range(depth):
        lambda_init = 0.8 - 0.6 * math.exp(-0.3 * layer)
        mod3 = _ada_mod(c, w_ada[layer], b_ada[layer]).reshape(batch, 6, d)
        q, k, v, yb, gates = _in_proj(
            xf, positions, mod3, norm1_g[layer].reshape(1, d), w_in[layer].astype(BF16),
            w_gate[layer].astype(BF16), b_gate[layer].reshape(1, 2 * d), conv_w[layer], seq)
        attn = _diff_attn(q, k, v,
                          lambda_q1[layer].reshape(1, HEAD_DIM), lambda_k1[layer].reshape(1, HEAD_DIM),
                          lambda_q2[layer].reshape(1, HEAD_DIM), lambda_k2[layer].reshape(1, HEAD_DIM),
                          subln_g[layer].reshape(1, HEAD_WIDTH), batch, seq, lambda_init)
        x1, h2 = _mix_out(attn, yb, gates, xf, mod3,
                          w_attn_o[layer].astype(BF16), w_conv_o[layer].astype(BF16),
                          w_out[layer].astype(BF16), norm2_g[layer].reshape(1, d), seq)
        last = layer == depth - 1
        assert last, "conv_ffn fuses the final RMSNorm: DEPTH > 1 needs an un-normalised variant"
        xf = _conv_ffn(h2, x1, mod3, w_up[layer].astype(BF16), ffn_conv_w[layer],
                       ffn_conv_b[layer].reshape(1, -1), w_down[layer].astype(BF16),
                       final_g.reshape(1, d), seq)
    return xf.reshape(batch, seq, d)
```

```python
import functools
import math

import jax
import jax.numpy as jnp
from jax import lax
from jax.experimental import pallas as pl
from jax.experimental.pallas import tpu as pltpu

F32 = jnp.float32
BF16 = jnp.bfloat16

HEAD_DIM = 64
HEAD_WIDTH = 2 * HEAD_DIM
ROPE_THETA = 10000.0
EPS = 1e-6
SUBLN_EPS = 1e-5
LANES = 128
F32_SUBLANES = 8
BF16_SUBLANES = 16
MXU_TILE = 256
VMEM_LIMIT = 56 * 1024 * 1024

ROW_TILE = 512
Q_TILE = 256
HEADS_PER_STEP = 4
K_CHUNK = 256
FF_CHUNKS = 1
IN_ROW_GROUPS = 2
MIX_ROW_TILE = 1024
FFN_ROW_TILE = 1024
MIX_ROW_GROUPS = 2


def _params(*sem):
    return pltpu.CompilerParams(dimension_semantics=sem, vmem_limit_bytes=VMEM_LIMIT)


def _rms(x, g, eps):
    return x * lax.rsqrt(jnp.mean(x * x, axis=-1, keepdims=True) + eps) * g


def _resident(shape):
    return pl.BlockSpec(shape, lambda *_: (0,) * len(shape), pipeline_mode=pl.Buffered(1))


def _halo_specs(ts, width, rows, halo):
    per_tile = ts // halo
    last = rows // halo - 1
    prev = pl.BlockSpec((halo, width), lambda i: (jnp.maximum(i * per_tile - 1, 0), 0))
    nxt = pl.BlockSpec((halo, width), lambda i: (jnp.minimum((i + 1) * per_tile, last), 0))
    return prev, nxt


def _conv3_rows(ext, halo, ts, w_ref, cols=slice(None)):
    n_ext = ts + 2 * halo
    before = pltpu.roll(ext, 1, 0)[halo:halo + ts]
    after = pltpu.roll(ext, n_ext - 1, 0)[halo:halo + ts]
    return (before * w_ref[0:1, cols] + ext[halo:halo + ts] * w_ref[1:2, cols]
            + after * w_ref[2:3, cols])


def _ada_kernel(c_ref, w_ref, b_ref, o_ref):
    c = c_ref[...]
    c_act = (c * jax.nn.sigmoid(c)).astype(BF16)
    o_ref[...] = jnp.dot(c_act, w_ref[...].astype(BF16), preferred_element_type=F32) + b_ref[...]


def _ada_mod(c, w_ada, b_ada):
    bsz, d = c.shape
    n = w_ada.shape[1]
    return pl.pallas_call(
        _ada_kernel,
        grid=(n // d,),
        in_specs=[pl.BlockSpec((bsz, d), lambda j: (0, 0)),
                  pl.BlockSpec((d, d), lambda j: (0, j)),
                  pl.BlockSpec((1, d), lambda j: (0, j))],
        out_specs=pl.BlockSpec((bsz, d), lambda j: (0, j)),
        out_shape=jax.ShapeDtypeStruct((bsz, n), F32),
        compiler_params=_params("parallel"),
        name="ada_mod",
    )(c, w_ada, b_ada.reshape(1, n))


def _rope(t, cos, sin_signed):
    lane = lax.broadcasted_iota(jnp.int32, (1, LANES), 1)
    first_half = (lane % HEAD_DIM) < (HEAD_DIM // 2)
    out = []
    for cb in range(t.shape[1] // LANES):
        blk = t[:, cb * LANES:(cb + 1) * LANES]
        upper = pltpu.roll(blk, LANES - HEAD_DIM // 2, 1)
        lower = pltpu.roll(blk, HEAD_DIM // 2, 1)
        partner = jnp.where(first_half, upper, lower)
        out.append(blk * cos + partner * sin_signed)
    return jnp.concatenate(out, axis=1)


def _rope_tables(pos_row, inv_col):
    ang = inv_col * pos_row.astype(F32)
    cos_t, sin_t = jnp.cos(ang), jnp.sin(ang)
    reps = LANES // HEAD_DIM
    cos = jnp.concatenate([cos_t, cos_t] * reps, axis=0).T
    sin = jnp.concatenate([-sin_t, sin_t] * reps, axis=0).T
    return cos, sin


def _in_proj_kernel(x_ref, x_prev_ref, x_next_ref, pos_ref, inv_ref, mod_ref, g_ref, win_ref, wg_ref,
                    bg_ref, cw_ref, q_ref, k_ref, v_ref, yb_ref, gate_ref, *, d, tiles_per_seq):
    gain = g_ref[...] * (1.0 + mod_ref[0, 1:2, :])

    def normed(x):
        return _rms(x, gain, EPS) + mod_ref[0, 0:1, :]

    def proj(h, j):
        return jnp.dot(h, win_ref[:, j * d:(j + 1) * d], preferred_element_type=F32)

    cos_all, sin_all = _rope_tables(pos_ref[0], inv_ref[...])
    ts = x_ref.shape[0]
    rows = ts // IN_ROW_GROUPS
    groups = [slice(r * rows, (r + 1) * rows) for r in range(IN_ROW_GROUPS)]
    h_rows = [normed(x_ref[sl, :]) for sl in groups]

    t = pl.program_id(0) % tiles_per_seq
    halo = x_prev_ref.shape[0]
    h_prev = jnp.where(t == 0, 0.0, normed(x_prev_ref[...]))
    h_next = jnp.where(t == tiles_per_seq - 1, 0.0, normed(x_next_ref[...]))
    h_ext = jnp.concatenate([h_prev] + h_rows + [h_next], axis=0).astype(BF16)
    conv = _conv3_rows(proj(h_ext, 4) * proj(h_ext, 5), halo, ts, cw_ref)

    for sl, h32 in zip(groups, h_rows):
        h = h32.astype(BF16)
        cos, sin = cos_all[sl, :], sin_all[sl, :]
        q_ref[sl, :] = (_rope(proj(h, 0), cos, sin) * (HEAD_DIM ** -0.5 * math.log2(math.e))).astype(BF16)
        k_ref[sl, :] = _rope(proj(h, 1), cos, sin).astype(BF16)
        v_ref[sl, :] = proj(h, 2).astype(BF16)
        yb_ref[sl, :] = (proj(h, 3) * conv[sl, :]).astype(BF16)
        for j in range(2):
            cols = slice(j * d, (j + 1) * d)
            z = jnp.dot(h, wg_ref[:, cols], preferred_element_type=F32)
            gate_ref[sl, cols] = jax.nn.sigmoid(z + bg_ref[:, cols]).astype(BF16)


def _in_proj(x2, positions, mod3, norm_g, w_in, w_gate, b_gate, conv_w, seq):
    rows, d = x2.shape
    ts = ROW_TILE
    nb = seq // ts
    assert w_in.shape[1] == 6 * d and w_gate.shape[1] == 2 * d and seq % ts == 0
    half = HEAD_DIM // 2
    inv_freq = ROPE_THETA ** (-jnp.arange(0, HEAD_DIM, 2, dtype=F32) / HEAD_DIM)
    row_blk = lambda w: pl.BlockSpec((ts, w), lambda i: (i, 0))
    prev, nxt = _halo_specs(ts, d, rows, F32_SUBLANES)
    return pl.pallas_call(
        functools.partial(_in_proj_kernel, d=d, tiles_per_seq=nb),
        grid=(rows // ts,),
        in_specs=[row_blk(d), prev, nxt,
                  pl.BlockSpec((1, 1, ts), lambda i: (i, 0, 0)),
                  _resident((half, 1)),
                  pl.BlockSpec((1, 6, d), lambda i: (i // nb, 0, 0)),
                  _resident((1, d)), _resident((d, 6 * d)), _resident((d, 2 * d)),
                  _resident((1, 2 * d)), _resident((3, d))],
        out_specs=[row_blk(d)] * 4 + [row_blk(2 * d)],
        out_shape=[jax.ShapeDtypeStruct((rows, d), BF16)] * 4
                  + [jax.ShapeDtypeStruct((rows, 2 * d), BF16)],
        compiler_params=_params("parallel"),
        name="in_proj",
    )(x2, x2, x2, positions.reshape(rows // ts, 1, ts), inv_freq.reshape(half, 1), mod3, norm_g,
      w_in, w_gate, b_gate, conv_w)


def _head_cols(hh):
    return slice(hh * HEAD_WIDTH, (hh + 1) * HEAD_WIDTH)


def _attn_kernel(q_ref, k_ref, v_ref, lq1_ref, lk1_ref, lq2_ref, lk2_ref, g_ref, o_ref,
                 vt_ref, s0_ref, s1_ref, acc_ref, *, lambda_init, bq, kc):
    seq = q_ref.shape[0]
    n_heads = q_ref.shape[1] // HEAD_WIDTH
    nq, nk = seq // bq, seq // kc
    nt_dims = (((1,), (1,)), ((), ()))
    s_refs = (s0_ref, s1_ref)
    lam = (jnp.exp(jnp.sum(lq1_ref[...] * lk1_ref[...], axis=-1, keepdims=True))
           - jnp.exp(jnp.sum(lq2_ref[...] * lk2_ref[...], axis=-1, keepdims=True))
           + lambda_init)
    for hh in range(n_heads):
        for c in range(nk):
            vt_ref[hh, c, :HEAD_WIDTH, :] = v_ref[c * kc:(c + 1) * kc, _head_cols(hh)].T
            vt_ref[hh, c, HEAD_WIDTH:, :] = jnp.ones((BF16_SUBLANES, kc), BF16)
    lane = lax.broadcasted_iota(jnp.int32, (1, HEAD_WIDTH), 1)
    m_init = jnp.full((F32_SUBLANES, 2 * bq), -jnp.inf, F32)

    def tile_rows(t):
        return pl.ds(pl.multiple_of(t * bq, bq), bq)

    def masked_queries(hh, t):
        q = q_ref[tile_rows(t), _head_cols(hh)]
        zero = jnp.zeros_like(q)
        return jnp.concatenate([jnp.where(lane < HEAD_DIM, q, zero),
                                jnp.where(lane >= HEAD_DIM, q, zero)], axis=0)

    def finish(hh, t):
        r = 1.0 / acc_ref[HEAD_WIDTH:HEAD_WIDTH + 1, :]
        acc = acc_ref[:HEAD_WIDTH, :]
        o_t = acc[:, :bq] * r[:, :bq] - acc[:, bq:] * (r[:, bq:] * lam)
        o = _rms(o_t.T, g_ref[...], SUBLN_EPS) * (1.0 - lambda_init)
        o_ref[tile_rows(t), _head_cols(hh)] = o.astype(BF16)

    def tile_step(score, value, fin, score_slot, m_prev):
        if fin is not None:
            finish(*fin)
        qq = None if score is None else masked_queries(*score)
        m, acc = m_init, None
        for c in range(nk):
            if score is not None:
                s = lax.dot_general(k_ref[c * kc:(c + 1) * kc, _head_cols(score[0])], qq, nt_dims,
                                    preferred_element_type=F32)
                s_refs[score_slot][c] = s
                m = jnp.maximum(m, jnp.max(s.reshape(kc // F32_SUBLANES, F32_SUBLANES, 2 * bq), axis=0))
            if value is not None:
                e = jnp.exp2(s_refs[1 - score_slot][c] - m_prev)
                pv = jnp.dot(vt_ref[value[0], c], e.astype(BF16), preferred_element_type=F32)
                acc = pv if acc is None else acc + pv
        if value is not None:
            acc_ref[...] = acc
        return jnp.max(m, axis=0, keepdims=True)

    assert nq % 2 == 0 and nq >= 4
    m_prev = tile_step((0, 0), None, None, 0, None)
    m_prev = tile_step((0, 1), (0, 0), None, 1, m_prev)
    for hh in range(n_heads):
        def tile_pair(u, m_prev, hh=hh):
            m_even = tile_step((hh, 2 * u + 2), (hh, 2 * u + 1), (hh, 2 * u), 0, m_prev)
            return tile_step((hh, 2 * u + 3), (hh, 2 * u + 2), (hh, 2 * u + 1), 1, m_even)

        m_prev = lax.fori_loop(0, nq // 2 - 1, tile_pair, m_prev)
        if hh + 1 < n_heads:
            m_prev = tile_step((hh + 1, 0), (hh, nq - 1), (hh, nq - 2), 0, m_prev)
            m_prev = tile_step((hh + 1, 1), (hh + 1, 0), (hh, nq - 1), 1, m_prev)
        else:
            tile_step(None, (hh, nq - 1), (hh, nq - 2), 0, m_prev)
            finish(hh, nq - 1)


def _diff_attn(q, k, v, lq1, lk1, lq2, lk2, subln_g, batch, seq, lambda_init):
    rows, width = q.shape
    heads = width // HEAD_WIDTH
    bq, kc, hp = Q_TILE, K_CHUNK, HEADS_PER_STEP
    assert seq % bq == 0 and seq % kc == 0 and heads % hp == 0
    lam_spec = pl.BlockSpec((1, HEAD_DIM), lambda b, h: (0, 0))
    head_spec = pl.BlockSpec((seq, hp * HEAD_WIDTH), lambda b, h: (b, h))
    return pl.pallas_call(
        functools.partial(_attn_kernel, lambda_init=lambda_init, bq=bq, kc=kc),
        grid=(batch, heads // hp),
        in_specs=[head_spec, head_spec, head_spec, lam_spec, lam_spec, lam_spec, lam_spec,
                  pl.BlockSpec((1, HEAD_WIDTH), lambda b, h: (0, 0))],
        out_specs=head_spec,
        out_shape=jax.ShapeDtypeStruct((rows, width), BF16),
        scratch_shapes=[pltpu.VMEM((hp, seq // kc, HEAD_WIDTH + BF16_SUBLANES, kc), BF16),
                        pltpu.VMEM((seq // kc, kc, 2 * bq), F32),
                        pltpu.VMEM((seq // kc, kc, 2 * bq), F32),
                        pltpu.VMEM((HEAD_WIDTH + BF16_SUBLANES, 2 * bq), F32)],
        compiler_params=_params("parallel", "parallel"),
        name="diff_attn",
    )(q, k, v, lq1, lk1, lq2, lk2, subln_g)


def _mix_kernel(attn_ref, yb_ref, gate_ref, x_ref, mod_ref, wa_ref, wc_ref, wo_ref, g2_ref,
                x1_ref, h2_ref, *, d):
    rows = attn_ref.shape[0] // MIX_ROW_GROUPS
    gain2 = g2_ref[...] * (1.0 + mod_ref[0, 4:5, :])
    for r in range(MIX_ROW_GROUPS):
        sl = slice(r * rows, (r + 1) * rows)
        y_a = jnp.dot(attn_ref[sl, :], wa_ref[...], preferred_element_type=F32)
        y_b = jnp.dot(yb_ref[sl, :], wc_ref[...], preferred_element_type=F32)
        merged = gate_ref[sl, :d] * y_a.astype(BF16) + gate_ref[sl, d:] * y_b.astype(BF16)
        mix = jnp.dot(merged, wo_ref[...], preferred_element_type=F32)
        x1 = x_ref[sl, :] + mod_ref[0, 2:3, :] * mix
        x1_ref[sl, :] = x1
        h2_ref[sl, :] = (_rms(x1, gain2, EPS) + mod_ref[0, 3:4, :]).astype(BF16)


def _mix_out(attn, yb, gates, x2, mod3, w_attn_o, w_conv_o, w_out, norm_g, seq):
    rows, d = x2.shape
    ts = MIX_ROW_TILE
    nb = seq // ts
    assert seq % ts == 0
    row_blk = lambda w: pl.BlockSpec((ts, w), lambda i: (i, 0))
    return pl.pallas_call(
        functools.partial(_mix_kernel, d=d),
        grid=(rows // ts,),
        in_specs=[row_blk(d), row_blk(d), row_blk(2 * d), row_blk(d),
                  pl.BlockSpec((1, 6, d), lambda i: (i // nb, 0, 0)),
                  _resident((d, d)), _resident((d, d)), _resident((d, d)), _resident((1, d))],
        out_specs=[row_blk(d), row_blk(d)],
        out_shape=[jax.ShapeDtypeStruct((rows, d), F32), jax.ShapeDtypeStruct((rows, d), BF16)],
        compiler_params=_params("parallel"),
        name="mix_out",
    )(attn, yb, gates, x2, mod3, w_attn_o, w_conv_o, w_out, norm_g)


def _ff_chunks(d_ff):
    tiles = d_ff // MXU_TILE
    assert tiles * MXU_TILE == d_ff and tiles >= FF_CHUNKS
    cuts = [MXU_TILE * ((c * tiles + FF_CHUNKS - 1) // FF_CHUNKS) for c in range(FF_CHUNKS)] + [d_ff]
    return list(zip(cuts[:-1], cuts[1:]))


def _ffn_kernel(h_ref, h_prev_ref, h_next_ref, x1_ref, mod_ref, wup_ref, cw_ref, cb_ref, wdn_ref,
                gf_ref, o_ref, *, d_ff, tiles_per_seq):
    t = pl.program_id(0) % tiles_per_seq
    ts = h_ref.shape[0]
    halo = h_prev_ref.shape[0]
    h = h_ref[...]
    h_prev = jnp.where(t == 0, jnp.zeros_like(h_prev_ref[...]), h_prev_ref[...])
    h_next = jnp.where(t == tiles_per_seq - 1, jnp.zeros_like(h_next_ref[...]), h_next_ref[...])
    h_ext = jnp.concatenate([h_prev, h, h_next], axis=0)
    ffn = None
    for lo, hi in _ff_chunks(d_ff):
        a_ext = jnp.dot(h_ext, wup_ref[:, lo:hi], preferred_element_type=F32)
        b = jnp.dot(h, wup_ref[:, d_ff + lo:d_ff + hi], preferred_element_type=F32)
        a = _conv3_rows(a_ext, halo, ts, cw_ref, slice(lo, hi)) + cb_ref[:, lo:hi]
        g = (a * jax.nn.sigmoid(a) * b).astype(BF16)
        part = jnp.dot(g, wdn_ref[lo:hi, :], preferred_element_type=F32)
        ffn = part if ffn is None else ffn + part
    x2 = x1_ref[...] + mod_ref[0, 5:6, :] * ffn
    o_ref[...] = _rms(x2, gf_ref[...], EPS)


def _conv_ffn(h2, x1, mod3, w_up, conv_w, conv_b, w_down, final_g, seq):
    rows, d = x1.shape
    d_ff = w_down.shape[0]
    ts = FFN_ROW_TILE
    nb = seq // ts
    assert seq % ts == 0
    row_blk = pl.BlockSpec((ts, d), lambda i: (i, 0))
    prev, nxt = _halo_specs(ts, d, rows, BF16_SUBLANES)
    return pl.pallas_call(
        functools.partial(_ffn_kernel, d_ff=d_ff, tiles_per_seq=nb),
        grid=(rows // ts,),
        in_specs=[row_blk, prev, nxt, row_blk,
                  pl.BlockSpec((1, 6, d), lambda i: (i // nb, 0, 0)),
                  _resident((d, 2 * d_ff)), _resident((3, d_ff)), _resident((1, d_ff)),
                  _resident((d_ff, d)), _resident((1, d))],
        out_specs=row_blk,
        out_shape=jax.ShapeDtypeStruct((rows, d), F32),
        compiler_params=_params("parallel"),
        name="conv_ffn",
    )(h2, h2, h2, x1, mod3, w_up, conv_w, conv_b, w_down, final_g)


def kernel(x, c, positions, w_ada, b_ada, norm1_g, w_in, conv_w, lambda_q1, lambda_k1, lambda_q2,
           lambda_k2, subln_g, w_attn_o, w_conv_o, w_gate, b_gate, w_out, norm2_g, w_up,
           ffn_conv_w, ffn_conv_b, w_down, final_g):
    batch, seq, d = x.shape
    depth = w_in.shape[0]
    rows = batch * seq
    xf = x.reshape(rows, d)
    for layer in range(depth):
        lambda_init = 0.8 - 0.6 * math.exp(-0.3 * layer)
        mod3 = _ada_mod(c, w_ada[layer], b_ada[layer]).reshape(batch, 6, d)
        q, k, v, yb, gates = _in_proj(
            xf, positions, mod3, norm1_g[layer].reshape(1, d), w_in[layer].astype(BF16),
            w_gate[layer].astype(BF16), b_gate[layer].reshape(1, 2 * d), conv_w[layer], seq)
        attn = _diff_attn(q, k, v,
                          lambda_q1[layer].reshape(1, HEAD_DIM), lambda_k1[layer].reshape(1, HEAD_DIM),
                          lambda_q2[layer].reshape(1, HEAD_DIM), lambda_k2[layer].reshape(1, HEAD_DIM),
                          subln_g[layer].reshape(1, HEAD_WIDTH), batch, seq, lambda_init)
        x1, h2 = _mix_out(attn, yb, gates, xf, mod3,
                          w_attn_o[layer].astype(BF16), w_conv_o[layer].astype(BF16),
                          w_out[layer].astype(BF16), norm2_g[layer].reshape(1, d), seq)
        last = layer == depth - 1
        assert last, "conv_ffn fuses the final RMSNorm: DEPTH > 1 needs an un-normalised variant"
        xf = _conv_ffn(h2, x1, mod3, w_up[layer].astype(BF16), ffn_conv_w[layer],
                       ffn_conv_b[layer].reshape(1, -1), w_down[layer].astype(BF16),
                       final_g.reshape(1, d), seq)
    return xf.reshape(batch, seq, d)
```

```python
import functools
import math

import jax
import jax.numpy as jnp
from jax import lax
from jax.experimental import pallas as pl
from jax.experimental.pallas import tpu as pltpu

F32 = jnp.float32
BF16 = jnp.bfloat16

HEAD_DIM = 64
HEAD_WIDTH = 2 * HEAD_DIM
ROPE_THETA = 10000.0
EPS = 1e-6
SUBLN_EPS = 1e-5
LANES = 128
F32_SUBLANES = 8
BF16_SUBLANES = 16
MXU_TILE = 256
VMEM_LIMIT = 56 * 1024 * 1024

ROW_TILE = 512
Q_TILE = 256
HEADS_PER_STEP = 2
K_CHUNK = 256
FF_CHUNKS = 1
IN_ROW_GROUPS = 2
MIX_ROW_TILE = 1024
FFN_ROW_TILE = 1024
MIX_ROW_GROUPS = 2


def _params(*sem):
    return pltpu.CompilerParams(dimension_semantics=sem, vmem_limit_bytes=VMEM_LIMIT)


def _rms(x, g, eps):
    return x * lax.rsqrt(jnp.mean(x * x, axis=-1, keepdims=True) + eps) * g


def _resident(shape):
    return pl.BlockSpec(shape, lambda *_: (0,) * len(shape), pipeline_mode=pl.Buffered(1))


def _halo_specs(ts, width, rows, halo):
    per_tile = ts // halo
    last = rows // halo - 1
    prev = pl.BlockSpec((halo, width), lambda i: (jnp.maximum(i * per_tile - 1, 0), 0))
    nxt = pl.BlockSpec((halo, width), lambda i: (jnp.minimum((i + 1) * per_tile, last), 0))
    return prev, nxt


def _conv3_rows(ext, halo, ts, w_ref, cols=slice(None)):
    n_ext = ts + 2 * halo
    before = pltpu.roll(ext, 1, 0)[halo:halo + ts]
    after = pltpu.roll(ext, n_ext - 1, 0)[halo:halo + ts]
    return (before * w_ref[0:1, cols] + ext[halo:halo + ts] * w_ref[1:2, cols]
            + after * w_ref[2:3, cols])


def _ada_kernel(c_ref, w_ref, b_ref, o_ref):
    c = c_ref[...]
    c_act = (c * jax.nn.sigmoid(c)).astype(BF16)
    o_ref[...] = jnp.dot(c_act, w_ref[...].astype(BF16), preferred_element_type=F32) + b_ref[...]


def _ada_mod(c, w_ada, b_ada):
    bsz, d = c.shape
    n = w_ada.shape[1]
    return pl.pallas_call(
        _ada_kernel,
        grid=(n // d,),
        in_specs=[pl.BlockSpec((bsz, d), lambda j: (0, 0)),
                  pl.BlockSpec((d, d), lambda j: (0, j)),
                  pl.BlockSpec((1, d), lambda j: (0, j))],
        out_specs=pl.BlockSpec((bsz, d), lambda j: (0, j)),
        out_shape=jax.ShapeDtypeStruct((bsz, n), F32),
        compiler_params=_params("parallel"),
        name="ada_mod",
    )(c, w_ada, b_ada.reshape(1, n))


def _rope(t, cos, sin_signed):
    lane = lax.broadcasted_iota(jnp.int32, (1, LANES), 1)
    first_half = (lane % HEAD_DIM) < (HEAD_DIM // 2)
    out = []
    for cb in range(t.shape[1] // LANES):
        blk = t[:, cb * LANES:(cb + 1) * LANES]
        upper = pltpu.roll(blk, LANES - HEAD_DIM // 2, 1)
        lower = pltpu.roll(blk, HEAD_DIM // 2, 1)
        partner = jnp.where(first_half, upper, lower)
        out.append(blk * cos + partner * sin_signed)
    return jnp.concatenate(out, axis=1)


def _rope_tables(pos_row, inv_col):
    ang = inv_col * pos_row.astype(F32)
    cos_t, sin_t = jnp.cos(ang), jnp.sin(ang)
    reps = LANES // HEAD_DIM
    cos = jnp.concatenate([cos_t, cos_t] * reps, axis=0).T
    sin = jnp.concatenate([-sin_t, sin_t] * reps, axis=0).T
    return cos, sin


def _in_proj_kernel(x_ref, x_prev_ref, x_next_ref, pos_ref, inv_ref, mod_ref, g_ref, win_ref, wg_ref,
                    bg_ref, cw_ref, q_ref, k_ref, v_ref, yb_ref, gate_ref, *, d, tiles_per_seq):
    gain = g_ref[...] * (1.0 + mod_ref[0, 1:2, :])

    def normed(x):
        return _rms(x, gain, EPS) + mod_ref[0, 0:1, :]

    def proj(h, j):
        return jnp.dot(h, win_ref[:, j * d:(j + 1) * d], preferred_element_type=F32)

    cos_all, sin_all = _rope_tables(pos_ref[0], inv_ref[...])
    ts = x_ref.shape[0]
    rows = ts // IN_ROW_GROUPS
    groups = [slice(r * rows, (r + 1) * rows) for r in range(IN_ROW_GROUPS)]
    h_rows = [normed(x_ref[sl, :]) for sl in groups]

    t = pl.program_id(0) % tiles_per_seq
    halo = x_prev_ref.shape[0]
    h_prev = jnp.where(t == 0, 0.0, normed(x_prev_ref[...]))
    h_next = jnp.where(t == tiles_per_seq - 1, 0.0, normed(x_next_ref[...]))
    h_ext = jnp.concatenate([h_prev] + h_rows + [h_next], axis=0).astype(BF16)
    conv = _conv3_rows(proj(h_ext, 4) * proj(h_ext, 5), halo, ts, cw_ref)

    for sl, h32 in zip(groups, h_rows):
        h = h32.astype(BF16)
        cos, sin = cos_all[sl, :], sin_all[sl, :]
        q_ref[sl, :] = (_rope(proj(h, 0), cos, sin) * (HEAD_DIM ** -0.5 * math.log2(math.e))).astype(BF16)
        k_ref[sl, :] = _rope(proj(h, 1), cos, sin).astype(BF16)
        v_ref[sl, :] = proj(h, 2).astype(BF16)
        yb_ref[sl, :] = (proj(h, 3) * conv[sl, :]).astype(BF16)
        for j in range(2):
            cols = slice(j * d, (j + 1) * d)
            z = jnp.dot(h, wg_ref[:, cols], preferred_element_type=F32)
            gate_ref[sl, cols] = jax.nn.sigmoid(z + bg_ref[:, cols]).astype(BF16)


def _in_proj(x2, positions, mod3, norm_g, w_in, w_gate, b_gate, conv_w, seq):
    rows, d = x2.shape
    ts = ROW_TILE
    nb = seq // ts
    assert w_in.shape[1] == 6 * d and w_gate.shape[1] == 2 * d and seq % ts == 0
    half = HEAD_DIM // 2
    inv_freq = ROPE_THETA ** (-jnp.arange(0, HEAD_DIM, 2, dtype=F32) / HEAD_DIM)
    row_blk = lambda w: pl.BlockSpec((ts, w), lambda i: (i, 0))
    prev, nxt = _halo_specs(ts, d, rows, F32_SUBLANES)
    return pl.pallas_call(
        functools.partial(_in_proj_kernel, d=d, tiles_per_seq=nb),
        grid=(rows // ts,),
        in_specs=[row_blk(d), prev, nxt,
                  pl.BlockSpec((1, 1, ts), lambda i: (i, 0, 0)),
                  _resident((half, 1)),
                  pl.BlockSpec((1, 6, d), lambda i: (i // nb, 0, 0)),
                  _resident((1, d)), _resident((d, 6 * d)), _resident((d, 2 * d)),
                  _resident((1, 2 * d)), _resident((3, d))],
        out_specs=[row_blk(d)] * 4 + [row_blk(2 * d)],
        out_shape=[jax.ShapeDtypeStruct((rows, d), BF16)] * 4
                  + [jax.ShapeDtypeStruct((rows, 2 * d), BF16)],
        compiler_params=_params("parallel"),
        name="in_proj",
    )(x2, x2, x2, positions.reshape(rows // ts, 1, ts), inv_freq.reshape(half, 1), mod3, norm_g,
      w_in, w_gate, b_gate, conv_w)


def _head_cols(hh):
    return slice(hh * HEAD_WIDTH, (hh + 1) * HEAD_WIDTH)


def _attn_kernel(q_ref, k_ref, v_ref, lq1_ref, lk1_ref, lq2_ref, lk2_ref, g_ref, o_ref,
                 vt_ref, s0_ref, s1_ref, acc_ref, *, lambda_init, bq, kc):
    seq = q_ref.shape[0]
    n_heads = q_ref.shape[1] // HEAD_WIDTH
    nq, nk = seq // bq, seq // kc
    nt_dims = (((1,), (1,)), ((), ()))
    s_refs = (s0_ref, s1_ref)
    lam = (jnp.exp(jnp.sum(lq1_ref[...] * lk1_ref[...], axis=-1, keepdims=True))
           - jnp.exp(jnp.sum(lq2_ref[...] * lk2_ref[...], axis=-1, keepdims=True))
           + lambda_init)
    for hh in range(n_heads):
        for c in range(nk):
            vt_ref[hh, c, :HEAD_WIDTH, :] = v_ref[c * kc:(c + 1) * kc, _head_cols(hh)].T
            vt_ref[hh, c, HEAD_WIDTH:, :] = jnp.ones((BF16_SUBLANES, kc), BF16)
    lane = lax.broadcasted_iota(jnp.int32, (1, HEAD_WIDTH), 1)
    m_init = jnp.full((F32_SUBLANES, 2 * bq), -jnp.inf, F32)

    def tile_rows(t):
        return pl.ds(pl.multiple_of(t * bq, bq), bq)

    def masked_queries(hh, t):
        q = q_ref[tile_rows(t), _head_cols(hh)]
        zero = jnp.zeros_like(q)
        return jnp.concatenate([jnp.where(lane < HEAD_DIM, q, zero),
                                jnp.where(lane >= HEAD_DIM, q, zero)], axis=0)

    def finish(hh, t):
        r = 1.0 / acc_ref[HEAD_WIDTH:HEAD_WIDTH + 1, :]
        acc = acc_ref[:HEAD_WIDTH, :]
        o_t = acc[:, :bq] * r[:, :bq] - acc[:, bq:] * (r[:, bq:] * lam)
        o = _rms(o_t.T, g_ref[...], SUBLN_EPS) * (1.0 - lambda_init)
        o_ref[tile_rows(t), _head_cols(hh)] = o.astype(BF16)

    def tile_step(score, value, fin, score_slot, m_prev):
        if fin is not None:
            finish(*fin)
        qq = None if score is None else masked_queries(*score)
        m, acc = m_init, None
        for c in range(nk):
            if score is not None:
                s = lax.dot_general(k_ref[c * kc:(c + 1) * kc, _head_cols(score[0])], qq, nt_dims,
                                    preferred_element_type=F32)
                s_refs[score_slot][c] = s
                m = jnp.maximum(m, jnp.max(s.reshape(kc // F32_SUBLANES, F32_SUBLANES, 2 * bq), axis=0))
            if value is not None:
                e = jnp.exp2(s_refs[1 - score_slot][c] - m_prev)
                pv = jnp.dot(vt_ref[value[0], c], e.astype(BF16), preferred_element_type=F32)
                acc = pv if acc is None else acc + pv
        if value is not None:
            acc_ref[...] = acc
        return jnp.max(m, axis=0, keepdims=True)

    assert nq % 2 == 0 and nq >= 4
    m_prev = tile_step((0, 0), None, None, 0, None)
    m_prev = tile_step((0, 1), (0, 0), None, 1, m_prev)
    for hh in range(n_heads):
        def tile_pair(u, m_prev, hh=hh):
            m_even = tile_step((hh, 2 * u + 2), (hh, 2 * u + 1), (hh, 2 * u), 0, m_prev)
            return tile_step((hh, 2 * u + 3), (hh, 2 * u + 2), (hh, 2 * u + 1), 1, m_even)

        m_prev = lax.fori_loop(0, nq // 2 - 1, tile_pair, m_prev)
        if hh + 1 < n_heads:
            m_prev = tile_step((hh + 1, 0), (hh, nq - 1), (hh, nq - 2), 0, m_prev)
            m_prev = tile_step((hh + 1, 1), (hh + 1, 0), (hh, nq - 1), 1, m_prev)
        else:
            tile_step(None, (hh, nq - 1), (hh, nq - 2), 0, m_prev)
            finish(hh, nq - 1)


def _diff_attn(q, k, v, lq1, lk1, lq2, lk2, subln_g, batch, seq, lambda_init):
    rows, width = q.shape
    heads = width // HEAD_WIDTH
    bq, kc, hp = Q_TILE, K_CHUNK, HEADS_PER_STEP
    assert seq % bq == 0 and seq % kc == 0 and heads % hp == 0
    lam_spec = pl.BlockSpec((1, HEAD_DIM), lambda b, h: (0, 0))
    head_spec = pl.BlockSpec((seq, hp * HEAD_WIDTH), lambda b, h: (b, h))
    return pl.pallas_call(
        functools.partial(_attn_kernel, lambda_init=lambda_init, bq=bq, kc=kc),
        grid=(batch, heads // hp),
        in_specs=[head_spec, head_spec, head_spec, lam_spec, lam_spec, lam_spec, lam_spec,
                  pl.BlockSpec((1, HEAD_WIDTH), lambda b, h: (0, 0))],
        out_specs=head_spec,
        out_shape=jax.ShapeDtypeStruct((rows, width), BF16),
        scratch_shapes=[pltpu.VMEM((hp, seq // kc, HEAD_WIDTH + BF16_SUBLANES, kc), BF16),
                        pltpu.VMEM((seq // kc, kc, 2 * bq), F32),
                        pltpu.VMEM((seq // kc, kc, 2 * bq), F32),
                        pltpu.VMEM((HEAD_WIDTH + BF16_SUBLANES, 2 * bq), F32)],
        compiler_params=_params("parallel", "parallel"),
        name="diff_attn",
    )(q, k, v, lq1, lk1, lq2, lk2, subln_g)


def _mix_kernel(attn_ref, yb_ref, gate_ref, x_ref, mod_ref, wa_ref, wc_ref, wo_ref, g2_ref,
                x1_ref, h2_ref, *, d):
    rows = attn_ref.shape[0] // MIX_ROW_GROUPS
    gain2 = g2_ref[...] * (1.0 + mod_ref[0, 4:5, :])
    for r in range(MIX_ROW_GROUPS):
        sl = slice(r * rows, (r + 1) * rows)
        y_a = jnp.dot(attn_ref[sl, :], wa_ref[...], preferred_element_type=F32)
        y_b = jnp.dot(yb_ref[sl, :], wc_ref[...], preferred_element_type=F32)
        merged = gate_ref[sl, :d] * y_a.astype(BF16) + gate_ref[sl, d:] * y_b.astype(BF16)
        mix = jnp.dot(merged, wo_ref[...], preferred_element_type=F32)
        x1 = x_ref[sl, :] + mod_ref[0, 2:3, :] * mix
        x1_ref[sl, :] = x1
        h2_ref[sl, :] = (_rms(x1, gain2, EPS) + mod_ref[0, 3:4, :]).astype(BF16)


def _mix_out(attn, yb, gates, x2, mod3, w_attn_o, w_conv_o, w_out, norm_g, seq):
    rows, d = x2.shape
    ts = MIX_ROW_TILE
    nb = seq // ts
    assert seq % ts == 0
    row_blk = lambda w: pl.BlockSpec((ts, w), lambda i: (i, 0))
    return pl.pallas_call(
        functools.partial(_mix_kernel, d=d),
        grid=(rows // ts,),
        in_specs=[row_blk(d), row_blk(d), row_blk(2 * d), row_blk(d),
                  pl.BlockSpec((1, 6, d), lambda i: (i // nb, 0, 0)),
                  _resident((d, d)), _resident((d, d)), _resident((d, d)), _resident((1, d))],
        out_specs=[row_blk(d), row_blk(d)],
        out_shape=[jax.ShapeDtypeStruct((rows, d), F32), jax.ShapeDtypeStruct((rows, d), BF16)],
        compiler_params=_params("parallel"),
        name="mix_out",
    )(attn, yb, gates, x2, mod3, w_attn_o, w_conv_o, w_out, norm_g)


def _ff_chunks(d_ff):
    tiles = d_ff // MXU_TILE
    assert tiles * MXU_TILE == d_ff and tiles >= FF_CHUNKS
    cuts = [MXU_TILE * ((c * tiles + FF_CHUNKS - 1) // FF_CHUNKS) for c in range(FF_CHUNKS)] + [d_ff]
    return list(zip(cuts[:-1], cuts[1:]))


def _ffn_kernel(h_ref, h_prev_ref, h_next_ref, x1_ref, mod_ref, wup_ref, cw_ref, cb_ref, wdn_ref,
                gf_ref, o_ref, *, d_ff, tiles_per_seq):
    t = pl.program_id(0) % tiles_per_seq
    ts = h_ref.shape[0]
    halo = h_prev_ref.shape[0]
    h = h_ref[...]
    h_prev = jnp.where(t == 0, jnp.zeros_like(h_prev_ref[...]), h_prev_ref[...])
    h_next = jnp.where(t == tiles_per_seq - 1, jnp.zeros_like(h_next_ref[...]), h_next_ref[...])
    h_ext = jnp.concatenate([h_prev, h, h_next], axis=0)
    ffn = None
    for lo, hi in _ff_chunks(d_ff):
        a_ext = jnp.dot(h_ext, wup_ref[:, lo:hi], preferred_element_type=F32)
        b = jnp.dot(h, wup_ref[:, d_ff + lo:d_ff + hi], preferred_element_type=F32)
        a = _conv3_rows(a_ext, halo, ts, cw_ref, slice(lo, hi)) + cb_ref[:, lo:hi]
        g = (a * jax.nn.sigmoid(a) * b).astype(BF16)
        part = jnp.dot(g, wdn_ref[lo:hi, :], preferred_element_type=F32)
        ffn = part if ffn is None else ffn + part
    x2 = x1_ref[...] + mod_ref[0, 5:6, :] * ffn
    o_ref[...] = _rms(x2, gf_ref[...], EPS)


def _conv_ffn(h2, x1, mod3, w_up, conv_w, conv_b, w_down, final_g, seq):
    rows, d = x1.shape
    d_ff = w_down.shape[0]
    ts = FFN_ROW_TILE
    nb = seq // ts
    assert seq % ts == 0
    row_blk = pl.BlockSpec((ts, d), lambda i: (i, 0))
    prev, nxt = _halo_specs(ts, d, rows, BF16_SUBLANES)
    return pl.pallas_call(
        functools.partial(_ffn_kernel, d_ff=d_ff, tiles_per_seq=nb),
        grid=(rows // ts,),
        in_specs=[row_blk, prev, nxt, row_blk,
                  pl.BlockSpec((1, 6, d), lambda i: (i // nb, 0, 0)),
                  _resident((d, 2 * d_ff)), _resident((3, d_ff)), _resident((1, d_ff)),
                  _resident((d_ff, d)), _resident((1, d))],
        out_specs=row_blk,
        out_shape=jax.ShapeDtypeStruct((rows, d), F32),
        compiler_params=_params("parallel"),
        name="conv_ffn",
    )(h2, h2, h2, x1, mod3, w_up, conv_w, conv_b, w_down, final_g)


def kernel(x, c, positions, w_ada, b_ada, norm1_g, w_in, conv_w, lambda_q1, lambda_k1, lambda_q2,
           lambda_k2, subln_g, w_attn_o, w_conv_o, w_gate, b_gate, w_out, norm2_g, w_up,
           ffn_conv_w, ffn_conv_b, w_down, final_g):
    batch, seq, d = x.shape
    depth = w_in.shape[0]
    rows = batch * seq
    xf = x.reshape(rows, d)
    for layer in range(depth):
        lambda_init = 0.8 - 0.6 * math.exp(-0.3 * layer)
        mod3 = _ada_mod(c, w_ada[layer], b_ada[layer]).reshape(batch, 6, d)
        q, k, v, yb, gates = _in_proj(
            xf, positions, mod3, norm1_g[layer].reshape(1, d), w_in[layer].astype(BF16),
            w_gate[layer].astype(BF16), b_gate[layer].reshape(1, 2 * d), conv_w[layer], seq)
        attn = _diff_attn(q, k, v,
                          lambda_q1[layer].reshape(1, HEAD_DIM), lambda_k1[layer].reshape(1, HEAD_DIM),
                          lambda_q2[layer].reshape(1, HEAD_DIM), lambda_k2[layer].reshape(1, HEAD_DIM),
                          subln_g[layer].reshape(1, HEAD_WIDTH), batch, seq, lambda_init)
        x1, h2 = _mix_out(attn, yb, gates, xf, mod3,
                          w_attn_o[layer].astype(BF16), w_conv_o[layer].astype(BF16),
                          w_out[layer].astype(BF16), norm2_g[layer].reshape(1, d), seq)
        last = layer == depth - 1
        assert last, "conv_ffn fuses the final RMSNorm: DEPTH > 1 needs an un-normalised variant"
        xf = _conv_ffn(h2, x1, mod3, w_up[layer].astype(BF16), ffn_conv_w[layer],
                       ffn_conv_b[layer].reshape(1, -1), w_down[layer].astype(BF16),
                       final_g.reshape(1, d), seq)
    return xf.reshape(batch, seq, d)
```

```python
import functools
import math

import jax
import jax.numpy as jnp
from jax import lax
from jax.experimental import pallas as pl
from jax.experimental.pallas import tpu as pltpu

F32 = jnp.float32
BF16 = jnp.bfloat16

HEAD_DIM = 64
HEAD_WIDTH = 2 * HEAD_DIM
ROPE_THETA = 10000.0
EPS = 1e-6
SUBLN_EPS = 1e-5
LANES = 128
F32_SUBLANES = 8
BF16_SUBLANES = 16
MXU_TILE = 256
VMEM_LIMIT = 56 * 1024 * 1024

ROW_TILE = 512
Q_TILE = 256
HEADS_PER_STEP = 2
K_CHUNK = 512
FF_CHUNKS = 1
IN_ROW_GROUPS = 2
IN_COL_BLOCK = 512
MIX_ROW_TILE = 1024
FFN_ROW_TILE = 1024
MIX_ROW_GROUPS = 2


def _params(*sem):
    return pltpu.CompilerParams(dimension_semantics=sem, vmem_limit_bytes=VMEM_LIMIT)


def _rms(x, g, eps):
    return x * lax.rsqrt(jnp.mean(x * x, axis=-1, keepdims=True) + eps) * g


def _resident(shape):
    return pl.BlockSpec(shape, lambda *_: (0,) * len(shape), pipeline_mode=pl.Buffered(1))


def _halo_specs(ts, width, rows, halo):
    per_tile = ts // halo
    last = rows // halo - 1
    prev = pl.BlockSpec((halo, width), lambda i: (jnp.maximum(i * per_tile - 1, 0), 0))
    nxt = pl.BlockSpec((halo, width), lambda i: (jnp.minimum((i + 1) * per_tile, last), 0))
    return prev, nxt


def _conv3_rows(ext, halo, ts, w_ref, cols=slice(None)):
    n_ext = ts + 2 * halo
    before = pltpu.roll(ext, 1, 0)[halo:halo + ts]
    after = pltpu.roll(ext, n_ext - 1, 0)[halo:halo + ts]
    return (before * w_ref[0:1, cols] + ext[halo:halo + ts] * w_ref[1:2, cols]
            + after * w_ref[2:3, cols])


def _ada_kernel(c_ref, w_ref, b_ref, o_ref):
    c = c_ref[...]
    c_act = (c * jax.nn.sigmoid(c)).astype(BF16)
    o_ref[...] = jnp.dot(c_act, w_ref[...].astype(BF16), preferred_element_type=F32) + b_ref[...]


def _ada_mod(c, w_ada, b_ada):
    bsz, d = c.shape
    n = w_ada.shape[1]
    return pl.pallas_call(
        _ada_kernel,
        grid=(n // d,),
        in_specs=[pl.BlockSpec((bsz, d), lambda j: (0, 0)),
                  pl.BlockSpec((d, d), lambda j: (0, j)),
                  pl.BlockSpec((1, d), lambda j: (0, j))],
        out_specs=pl.BlockSpec((bsz, d), lambda j: (0, j)),
        out_shape=jax.ShapeDtypeStruct((bsz, n), F32),
        compiler_params=_params("parallel"),
        name="ada_mod",
    )(c, w_ada, b_ada.reshape(1, n))


def _rope(t, cos, sin_signed):
    lane = lax.broadcasted_iota(jnp.int32, (1, LANES), 1)
    first_half = (lane % HEAD_DIM) < (HEAD_DIM // 2)
    out = []
    for cb in range(t.shape[1] // LANES):
        blk = t[:, cb * LANES:(cb + 1) * LANES]
        upper = pltpu.roll(blk, LANES - HEAD_DIM // 2, 1)
        lower = pltpu.roll(blk, HEAD_DIM // 2, 1)
        partner = jnp.where(first_half, upper, lower)
        out.append(blk * cos + partner * sin_signed)
    return jnp.concatenate(out, axis=1)


def _rope_tables(pos_row, inv_col):
    ang = inv_col * pos_row.astype(F32)
    cos_t, sin_t = jnp.cos(ang), jnp.sin(ang)
    reps = LANES // HEAD_DIM
    cos = jnp.concatenate([cos_t, cos_t] * reps, axis=0).T
    sin = jnp.concatenate([-sin_t, sin_t] * reps, axis=0).T
    return cos, sin


def _in_proj_kernel(x_ref, x_prev_ref, x_next_ref, pos_ref, inv_ref, mod_ref, g_ref, win_ref, wg_ref,
                    bg_ref, cw_ref, q_ref, k_ref, v_ref, yb_ref, gate_ref, *, d, tiles_per_seq):
    gain = g_ref[...] * (1.0 + mod_ref[0, 1:2, :])

    def normed(x):
        return _rms(x, gain, EPS) + mod_ref[0, 0:1, :]

    def proj(h, j, cb):
        lo = j * d + cb * IN_COL_BLOCK
        return jnp.dot(h, win_ref[:, lo:lo + IN_COL_BLOCK], preferred_element_type=F32)

    halves = [slice(cb * IN_COL_BLOCK, (cb + 1) * IN_COL_BLOCK) for cb in range(d // IN_COL_BLOCK)]

    cos_all, sin_all = _rope_tables(pos_ref[0], inv_ref[...])
    ts = x_ref.shape[0]
    rows = ts // IN_ROW_GROUPS
    groups = [slice(r * rows, (r + 1) * rows) for r in range(IN_ROW_GROUPS)]
    h_rows = [normed(x_ref[sl, :]) for sl in groups]

    t = pl.program_id(0) % tiles_per_seq
    halo = x_prev_ref.shape[0]
    h_prev = jnp.where(t == 0, 0.0, normed(x_prev_ref[...]))
    h_next = jnp.where(t == tiles_per_seq - 1, 0.0, normed(x_next_ref[...]))
    h_ext = jnp.concatenate([h_prev] + h_rows + [h_next], axis=0).astype(BF16)
    conv = [_conv3_rows(proj(h_ext, 4, cb) * proj(h_ext, 5, cb), halo, ts, cw_ref, cols)
            for cb, cols in enumerate(halves)]

    for sl, h32 in zip(groups, h_rows):
        h = h32.astype(BF16)
        cos, sin = cos_all[sl, :], sin_all[sl, :]
        for cb, cols in enumerate(halves):
            q_ref[sl, cols] = (_rope(proj(h, 0, cb), cos, sin)
                               * (HEAD_DIM ** -0.5 * math.log2(math.e))).astype(BF16)
            k_ref[sl, cols] = _rope(proj(h, 1, cb), cos, sin).astype(BF16)
            v_ref[sl, cols] = proj(h, 2, cb).astype(BF16)
            yb_ref[sl, cols] = (proj(h, 3, cb) * conv[cb][sl, :]).astype(BF16)
        for j in range(2):
            cols = slice(j * d, (j + 1) * d)
            z = jnp.dot(h, wg_ref[:, cols], preferred_element_type=F32)
            gate_ref[sl, cols] = jax.nn.sigmoid(z + bg_ref[:, cols]).astype(BF16)


def _in_proj(x2, positions, mod3, norm_g, w_in, w_gate, b_gate, conv_w, seq):
    rows, d = x2.shape
    ts = ROW_TILE
    nb = seq // ts
    assert w_in.shape[1] == 6 * d and w_gate.shape[1] == 2 * d and seq % ts == 0
    half = HEAD_DIM // 2
    inv_freq = ROPE_THETA ** (-jnp.arange(0, HEAD_DIM, 2, dtype=F32) / HEAD_DIM)
    row_blk = lambda w: pl.BlockSpec((ts, w), lambda i: (i, 0))
    prev, nxt = _halo_specs(ts, d, rows, F32_SUBLANES)
    return pl.pallas_call(
        functools.partial(_in_proj_kernel, d=d, tiles_per_seq=nb),
        grid=(rows // ts,),
        in_specs=[row_blk(d), prev, nxt,
                  pl.BlockSpec((1, 1, ts), lambda i: (i, 0, 0)),
                  _resident((half, 1)),
                  pl.BlockSpec((1, 6, d), lambda i: (i // nb, 0, 0)),
                  _resident((1, d)), _resident((d, 6 * d)), _resident((d, 2 * d)),
                  _resident((1, 2 * d)), _resident((3, d))],
        out_specs=[row_blk(d)] * 4 + [row_blk(2 * d)],
        out_shape=[jax.ShapeDtypeStruct((rows, d), BF16)] * 4
                  + [jax.ShapeDtypeStruct((rows, 2 * d), BF16)],
        compiler_params=_params("parallel"),
        name="in_proj",
    )(x2, x2, x2, positions.reshape(rows // ts, 1, ts), inv_freq.reshape(half, 1), mod3, norm_g,
      w_in, w_gate, b_gate, conv_w)


def _head_cols(hh):
    return slice(hh * HEAD_WIDTH, (hh + 1) * HEAD_WIDTH)


def _attn_kernel(q_ref, k_ref, v_ref, lq1_ref, lk1_ref, lq2_ref, lk2_ref, g_ref, o_ref,
                 vt_ref, s0_ref, s1_ref, acc_ref, *, lambda_init, bq, kc):
    seq = q_ref.shape[0]
    n_heads = q_ref.shape[1] // HEAD_WIDTH
    nq, nk = seq // bq, seq // kc
    nt_dims = (((1,), (1,)), ((), ()))
    s_refs = (s0_ref, s1_ref)
    lam = (jnp.exp(jnp.sum(lq1_ref[...] * lk1_ref[...], axis=-1, keepdims=True))
           - jnp.exp(jnp.sum(lq2_ref[...] * lk2_ref[...], axis=-1, keepdims=True))
           + lambda_init)
    for hh in range(n_heads):
        for c in range(nk):
            vt_ref[hh, c, :HEAD_WIDTH, :] = v_ref[c * kc:(c + 1) * kc, _head_cols(hh)].T
            vt_ref[hh, c, HEAD_WIDTH:, :] = jnp.ones((BF16_SUBLANES, kc), BF16)
    lane = lax.broadcasted_iota(jnp.int32, (1, HEAD_WIDTH), 1)
    m_init = jnp.full((F32_SUBLANES, 2 * bq), -jnp.inf, F32)

    def tile_rows(t):
        return pl.ds(pl.multiple_of(t * bq, bq), bq)

    def masked_queries(hh, t):
        q = q_ref[tile_rows(t), _head_cols(hh)]
        zero = jnp.zeros_like(q)
        return jnp.concatenate([jnp.where(lane < HEAD_DIM, q, zero),
                                jnp.where(lane >= HEAD_DIM, q, zero)], axis=0)

    def finish(hh, t):
        r = 1.0 / acc_ref[HEAD_WIDTH:HEAD_WIDTH + 1, :]
        acc = acc_ref[:HEAD_WIDTH, :]
        o_t = acc[:, :bq] * r[:, :bq] - acc[:, bq:] * (r[:, bq:] * lam)
        o = _rms(o_t.T, g_ref[...], SUBLN_EPS) * (1.0 - lambda_init)
        o_ref[tile_rows(t), _head_cols(hh)] = o.astype(BF16)

    def tile_step(score, value, fin, score_slot, m_prev):
        if fin is not None:
            finish(*fin)
        qq = None if score is None else masked_queries(*score)
        m, acc = m_init, None
        for c in range(nk):
            if score is not None:
                s = lax.dot_general(k_ref[c * kc:(c + 1) * kc, _head_cols(score[0])], qq, nt_dims,
                                    preferred_element_type=F32)
                s_refs[score_slot][c] = s
                m = jnp.maximum(m, jnp.max(s.reshape(kc // F32_SUBLANES, F32_SUBLANES, 2 * bq), axis=0))
            if value is not None:
                e = jnp.exp2(s_refs[1 - score_slot][c] - m_prev)
                pv = jnp.dot(vt_ref[value[0], c], e.astype(BF16), preferred_element_type=F32)
                acc = pv if acc is None else acc + pv
        if value is not None:
            acc_ref[...] = acc
        return jnp.max(m, axis=0, keepdims=True)

    assert nq % 2 == 0 and nq >= 4
    m_prev = tile_step((0, 0), None, None, 0, None)
    m_prev = tile_step((0, 1), (0, 0), None, 1, m_prev)
    for hh in range(n_heads):
        def tile_pair(u, m_prev, hh=hh):
            m_even = tile_step((hh, 2 * u + 2), (hh, 2 * u + 1), (hh, 2 * u), 0, m_prev)
            return tile_step((hh, 2 * u + 3), (hh, 2 * u + 2), (hh, 2 * u + 1), 1, m_even)

        m_prev = lax.fori_loop(0, nq // 2 - 1, tile_pair, m_prev)
        if hh + 1 < n_heads:
            m_prev = tile_step((hh + 1, 0), (hh, nq - 1), (hh, nq - 2), 0, m_prev)
            m_prev = tile_step((hh + 1, 1), (hh + 1, 0), (hh, nq - 1), 1, m_prev)
        else:
            tile_step(None, (hh, nq - 1), (hh, nq - 2), 0, m_prev)
            finish(hh, nq - 1)


def _diff_attn(q, k, v, lq1, lk1, lq2, lk2, subln_g, batch, seq, lambda_init):
    rows, width = q.shape
    heads = width // HEAD_WIDTH
    bq, kc, hp = Q_TILE, K_CHUNK, HEADS_PER_STEP
    assert seq % bq == 0 and seq % kc == 0 and heads % hp == 0
    lam_spec = pl.BlockSpec((1, HEAD_DIM), lambda b, h: (0, 0))
    head_spec = pl.BlockSpec((seq, hp * HEAD_WIDTH), lambda b, h: (b, h))
    return pl.pallas_call(
        functools.partial(_attn_kernel, lambda_init=lambda_init, bq=bq, kc=kc),
        grid=(batch, heads // hp),
        in_specs=[head_spec, head_spec, head_spec, lam_spec, lam_spec, lam_spec, lam_spec,
                  pl.BlockSpec((1, HEAD_WIDTH), lambda b, h: (0, 0))],
        out_specs=head_spec,
        out_shape=jax.ShapeDtypeStruct((rows, width), BF16),
        scratch_shapes=[pltpu.VMEM((hp, seq // kc, HEAD_WIDTH + BF16_SUBLANES, kc), BF16),
                        pltpu.VMEM((seq // kc, kc, 2 * bq), F32),
                        pltpu.VMEM((seq // kc, kc, 2 * bq), F32),
                        pltpu.VMEM((HEAD_WIDTH + BF16_SUBLANES, 2 * bq), F32)],
        compiler_params=_params("parallel", "parallel"),
        name="diff_attn",
    )(q, k, v, lq1, lk1, lq2, lk2, subln_g)


def _mix_kernel(attn_ref, yb_ref, gate_ref, x_ref, mod_ref, wa_ref, wc_ref, wo_ref, g2_ref,
                x1_ref, h2_ref, *, d):
    rows = attn_ref.shape[0] // MIX_ROW_GROUPS
    gain2 = g2_ref[...] * (1.0 + mod_ref[0, 4:5, :])
    for r in range(MIX_ROW_GROUPS):
        sl = slice(r * rows, (r + 1) * rows)
        y_a = jnp.dot(attn_ref[sl, :], wa_ref[...], preferred_element_type=F32)
        y_b = jnp.dot(yb_ref[sl, :], wc_ref[...], preferred_element_type=F32)
        merged = gate_ref[sl, :d] * y_a.astype(BF16) + gate_ref[sl, d:] * y_b.astype(BF16)
        mix = jnp.dot(merged, wo_ref[...], preferred_element_type=F32)
        x1 = x_ref[sl, :] + mod_ref[0, 2:3, :] * mix
        x1_ref[sl, :] = x1
        h2_ref[sl, :] = (_rms(x1, gain2, EPS) + mod_ref[0, 3:4, :]).astype(BF16)


def _mix_out(attn, yb, gates, x2, mod3, w_attn_o, w_conv_o, w_out, norm_g, seq):
    rows, d = x2.shape
    ts = MIX_ROW_TILE
    nb = seq // ts
    assert seq % ts == 0
    row_blk = lambda w: pl.BlockSpec((ts, w), lambda i: (i, 0))
    return pl.pallas_call(
        functools.partial(_mix_kernel, d=d),
        grid=(rows // ts,),
        in_specs=[row_blk(d), row_blk(d), row_blk(2 * d), row_blk(d),
                  pl.BlockSpec((1, 6, d), lambda i: (i // nb, 0, 0)),
                  _resident((d, d)), _resident((d, d)), _resident((d, d)), _resident((1, d))],
        out_specs=[row_blk(d), row_blk(d)],
        out_shape=[jax.ShapeDtypeStruct((rows, d), F32), jax.ShapeDtypeStruct((rows, d), BF16)],
        compiler_params=_params("parallel"),
        name="mix_out",
    )(attn, yb, gates, x2, mod3, w_attn_o, w_conv_o, w_out, norm_g)


def _ff_chunks(d_ff):
    tiles = d_ff // MXU_TILE
    assert tiles * MXU_TILE == d_ff and tiles >= FF_CHUNKS
    cuts = [MXU_TILE * ((c * tiles + FF_CHUNKS - 1) // FF_CHUNKS) for c in range(FF_CHUNKS)] + [d_ff]
    return list(zip(cuts[:-1], cuts[1:]))


def _ffn_kernel(h_ref, h_prev_ref, h_next_ref, x1_ref, mod_ref, wup_ref, cw_ref, cb_ref, wdn_ref,
                gf_ref, o_ref, *, d_ff, tiles_per_seq):
    t = pl.program_id(0) % tiles_per_seq
    ts = h_ref.shape[0]
    halo = h_prev_ref.shape[0]
    h = h_ref[...]
    h_prev = jnp.where(t == 0, jnp.zeros_like(h_prev_ref[...]), h_prev_ref[...])
    h_next = jnp.where(t == tiles_per_seq - 1, jnp.zeros_like(h_next_ref[...]), h_next_ref[...])
    h_ext = jnp.concatenate([h_prev, h, h_next], axis=0)
    ffn = None
    for lo, hi in _ff_chunks(d_ff):
        a_ext = jnp.dot(h_ext, wup_ref[:, lo:hi], preferred_element_type=F32)
        b = jnp.dot(h, wup_ref[:, d_ff + lo:d_ff + hi], preferred_element_type=F32)
        a = _conv3_rows(a_ext, halo, ts, cw_ref, slice(lo, hi)) + cb_ref[:, lo:hi]
        g = (a * jax.nn.sigmoid(a) * b).astype(BF16)
        part = jnp.dot(g, wdn_ref[lo:hi, :], preferred_element_type=F32)
        ffn = part if ffn is None else ffn + part
    x2 = x1_ref[...] + mod_ref[0, 5:6, :] * ffn
    o_ref[...] = _rms(x2, gf_ref[...], EPS)


def _conv_ffn(h2, x1, mod3, w_up, conv_w, conv_b, w_down, final_g, seq):
    rows, d = x1.shape
    d_ff = w_down.shape[0]
    ts = FFN_ROW_TILE
    nb = seq // ts
    assert seq % ts == 0
    row_blk = pl.BlockSpec((ts, d), lambda i: (i, 0))
    prev, nxt = _halo_specs(ts, d, rows, BF16_SUBLANES)
    return pl.pallas_call(
        functools.partial(_ffn_kernel, d_ff=d_ff, tiles_per_seq=nb),
        grid=(rows // ts,),
        in_specs=[row_blk, prev, nxt, row_blk,
                  pl.BlockSpec((1, 6, d), lambda i: (i // nb, 0, 0)),
                  _resident((d, 2 * d_ff)), _resident((3, d_ff)), _resident((1, d_ff)),
                  _resident((d_ff, d)), _resident((1, d))],
        out_specs=row_blk,
        out_shape=jax.ShapeDtypeStruct((rows, d), F32),
        compiler_params=_params("parallel"),
        name="conv_ffn",
    )(h2, h2, h2, x1, mod3, w_up, conv_w, conv_b, w_down, final_g)


def kernel(x, c, positions, w_ada, b_ada, norm1_g, w_in, conv_w, lambda_q1, lambda_k1, lambda_q2,
           lambda_k2, subln_g, w_attn_o, w_conv_o, w_gate, b_gate, w_out, norm2_g, w_up,
           ffn_conv_w, ffn_conv_b, w_down, final_g):
    batch, seq, d = x.shape
    depth = w_in.shape[0]
    rows = batch * seq
    xf = x.reshape(rows, d)
    for layer in range(depth):
        lambda_init = 0.8 - 0.6 * math.exp(-0.3 * layer)
        mod3 = _ada_mod(c, w_ada[layer], b_ada[layer]).reshape(batch, 6, d)
        q, k, v, yb, gates = _in_proj(
            xf, positions, mod3, norm1_g[layer].reshape(1, d), w_in[layer].astype(BF16),
            w_gate[layer].astype(BF16), b_gate[layer].reshape(1, 2 * d), conv_w[layer], seq)
        attn = _diff_attn(q, k, v,
                          lambda_q1[layer].reshape(1, HEAD_DIM), lambda_k1[layer].reshape(1, HEAD_DIM),
                          lambda_q2[layer].reshape(1, HEAD_DIM), lambda_k2[layer].reshape(1, HEAD_DIM),
                          subln_g[layer].reshape(1, HEAD_WIDTH), batch, seq, lambda_init)
        x1, h2 = _mix_out(attn, yb, gates, xf, mod3,
                          w_attn_o[layer].astype(BF16), w_conv_o[layer].astype(BF16),
                          w_out[layer].astype(BF16), norm2_g[layer].reshape(1, d), seq)
        last = layer == depth - 1
        assert last, "conv_ffn fuses the final RMSNorm: DEPTH > 1 needs an un-normalised variant"
        xf = _conv_ffn(h2, x1, mod3, w_up[layer].astype(BF16), ffn_conv_w[layer],
                       ffn_conv_b[layer].reshape(1, -1), w_down[layer].astype(BF16),
                       final_g.reshape(1, d), seq)
    return xf.reshape(batch, seq, d)
```
